```python
import jax, jax.numpy as jnp
from jax import lax
import numpy as np

D_MODEL = 1024
BATCH = 16
SEQ = 4096
DEPTH = 4
DEC_BATCH = 1
DEC_SEQ = 16384
PAST_LEN = 128

GRID_W = 64
NA_HEADS = 8
NA_HEAD_DIM = 64
NA_WIDTH = NA_HEADS * NA_HEAD_DIM
NA_WH = 8
NA_WW = 16
RET_HEADS = 4
RET_QK_DIM = 128
RET_V_DIM = 256
RET_QK_WIDTH = RET_HEADS * RET_QK_DIM
RET_V_WIDTH = RET_HEADS * RET_V_DIM
RET_CHUNK = 128
ROPE_BASE = 10000.0
D_FF = 2816
CONV_WIDTH = 3
EPS = 1e-6
SPLIT_SIZES = (NA_WIDTH, NA_WIDTH, NA_WIDTH,
               RET_QK_WIDTH, RET_QK_WIDTH, RET_V_WIDTH, RET_V_WIDTH,
               D_MODEL, D_MODEL)
D_IN = sum(SPLIT_SIZES)

kernel_name = "hybrid_natten_retnet_encoder"


def rms_norm(x, g):
    xf = x.astype(jnp.float32)
    y = xf * lax.rsqrt(jnp.mean(xf * xf, axis=-1, keepdims=True) + EPS)
    return (y * g.astype(jnp.float32)).astype(x.dtype)


def rope_tables(seq_len):
    inv_freq = ROPE_BASE ** (-jnp.arange(0, RET_QK_DIM, 2, dtype=jnp.float32) / RET_QK_DIM)
    ang = jnp.arange(seq_len, dtype=jnp.float32)[:, None] * inv_freq[None, :]
    return jnp.cos(ang)[:, None, :], jnp.sin(ang)[:, None, :]


def apply_rope(x, cos, sin):
    x1, x2 = jnp.split(x, 2, axis=-1)
    return jnp.concatenate([x1 * cos - x2 * sin, x1 * sin + x2 * cos], axis=-1)


def neighborhood_attention(q, k, v, rel_bias):
    B, S, H, dh = q.shape
    rows = S // GRID_W
    wh = min(NA_WH, rows)
    scale = dh ** -0.5
    q = q.reshape(B, rows, GRID_W, H, dh)
    k = k.reshape(B, rows, GRID_W, H, dh)
    v = v.reshape(B, rows, GRID_W, H, dh)
    col = np.arange(GRID_W)
    col_start = np.clip(col - NA_WW // 2, 0, GRID_W - NA_WW)
    col_idx = col_start[:, None] + np.arange(NA_WW)[None, :]
    col_bias_idx = (col_idx - col[:, None]) + (NA_WW - 1)

    def one_row(r):
        rs = jnp.clip(r - wh // 2, 0, rows - wh)
        q_r = lax.dynamic_index_in_dim(q, r, axis=1, keepdims=False)
        k_band = lax.dynamic_slice_in_dim(k, rs, wh, axis=1)
        v_band = lax.dynamic_slice_in_dim(v, rs, wh, axis=1)
        k_win = k_band[:, :, col_idx]
        v_win = v_band[:, :, col_idx]
        s = jnp.einsum('bqhd,bwqchd->bhqwc', q_r, k_win).astype(jnp.float32) * scale
        row_bias_idx = (rs + jnp.arange(wh) - r) + (NA_WH - 1)
        bias = rel_bias[:, row_bias_idx[:, None, None], col_bias_idx[None, :, :]]
        s = s + jnp.transpose(bias, (0, 2, 1, 3)).astype(jnp.float32)[None]
        p = jax.nn.softmax(s.reshape(B, H, GRID_W, wh * NA_WW), axis=-1)
        p = p.reshape(B, H, GRID_W, wh, NA_WW).astype(v.dtype)
        return jnp.einsum('bhqwc,bwqchd->bqhd', p, v_win)

    out = lax.map(one_row, jnp.arange(rows))
    return jnp.transpose(out, (1, 0, 2, 3, 4)).reshape(B, S, H * dh)


def retention_scan(q, k, v, log_gamma, inclusive):
    B, H, S, dk = q.shape
    dv = v.shape[-1]
    C = RET_CHUNK
    N = S // C
    qc = jnp.transpose(q.reshape(B, H, N, C, dk), (2, 0, 1, 3, 4))
    kc = jnp.transpose(k.reshape(B, H, N, C, dk), (2, 0, 1, 3, 4))
    vc = jnp.transpose(v.reshape(B, H, N, C, dv), (2, 0, 1, 3, 4))
    idx = jnp.arange(C, dtype=jnp.float32)
    diff = idx[:, None] - idx[None, :]
    mask = diff >= 0 if inclusive else diff > 0
    lg = log_gamma[:, None, None]
    d_intra = jnp.where(mask[None], jnp.exp(jnp.maximum(diff, 0.0)[None] * lg), 0.0)
    q_decay = jnp.exp((idx + 1.0)[None, :] * log_gamma[:, None])[None, :, :, None]
    k_decay = jnp.exp((C - 1.0 - idx)[None, :] * log_gamma[:, None])[None, :, :, None]
    chunk_decay = jnp.exp(C * log_gamma)[None, :, None, None]

    def step(state, inp):
        q_i, k_i, v_i = inp
        intra = jnp.einsum('bhid,bhjd->bhij', q_i, k_i) * d_intra[None]
        o = jnp.einsum('bhij,bhjv->bhiv', intra, v_i) + \
            jnp.einsum('bhid,bhdv->bhiv', q_i * q_decay, state)
        state = state * chunk_decay + jnp.einsum('bhjd,bhjv->bhdv', k_i * k_decay, v_i)
        return state, o

    state0 = jnp.zeros((B, H, dk, dv), jnp.float32)
    _, o = lax.scan(step, state0, (qc, kc, vc))
    return jnp.transpose(o, (1, 2, 0, 3, 4)).reshape(B, H, S, dv)


def bidirectional_retention(q, k, v, lg_fwd, lg_bwd):
    q = jnp.transpose(q, (0, 2, 1, 3))
    k = jnp.transpose(k, (0, 2, 1, 3))
    v = jnp.transpose(v, (0, 2, 1, 3))
    fwd = retention_scan(q, k, v, lg_fwd, True)
    bwd = retention_scan(q[:, :, ::-1], k[:, :, ::-1], v[:, :, ::-1], lg_bwd, False)[:, :, ::-1]
    return fwd + bwd


def head_group_norm(o, g):
    mu = jnp.mean(o, axis=-1, keepdims=True)
    var = jnp.mean(jnp.square(o - mu), axis=-1, keepdims=True)
    y = (o - mu) * lax.rsqrt(var + EPS)
    B, H, S, dv = o.shape
    y = jnp.transpose(y, (0, 2, 1, 3)).reshape(B, S, H * dv)
    return y * g.astype(jnp.float32)


def centred_depthwise_conv(u, w):
    up = jnp.pad(u, ((0, 0), (1, 1), (0, 0)))
    return up[:, :-2] * w[0] + up[:, 1:-1] * w[1] + up[:, 2:] * w[2]


def trunk(x, norm_mix_g, w_in, na_rel_bias, ret_decay_fwd, ret_decay_bwd, ret_norm_g,
          w_branch_attn, w_branch_ret, w_out, norm_ffn_g, w_up, ffn_conv_w, w_down, norm_final_g):
    B, S, _ = x.shape
    cos, sin = rope_tables(S)
    split_at = np.cumsum(SPLIT_SIZES)[:-1].tolist()
    for l in range(DEPTH):
        h = rms_norm(x, norm_mix_g[l])
        proj = h @ w_in[l]
        na_q, na_k, na_v, r_q, r_k, r_v, r_g, gate_a, gate_r = jnp.split(proj, split_at, axis=-1)
        a = neighborhood_attention(na_q.reshape(B, S, NA_HEADS, NA_HEAD_DIM),
                                   na_k.reshape(B, S, NA_HEADS, NA_HEAD_DIM),
                                   na_v.reshape(B, S, NA_HEADS, NA_HEAD_DIM),
                                   na_rel_bias[l])
        rq = apply_rope(r_q.astype(jnp.float32).reshape(B, S, RET_HEADS, RET_QK_DIM), cos, sin)
        rk = apply_rope(r_k.astype(jnp.float32).reshape(B, S, RET_HEADS, RET_QK_DIM), cos, sin) * (RET_QK_DIM ** -0.5)
        rv = r_v.astype(jnp.float32).reshape(B, S, RET_HEADS, RET_V_DIM)
        ro = bidirectional_retention(rq, rk, rv,
                                     jax.nn.log_sigmoid(ret_decay_fwd[l].astype(jnp.float32)),
                                     jax.nn.log_sigmoid(ret_decay_bwd[l].astype(jnp.float32)))
        ro = (jax.nn.silu(r_g.astype(jnp.float32)) * head_group_norm(ro, ret_norm_g[l])).astype(x.dtype)
        y_a = a @ w_branch_attn[l]
        y_r = ro @ w_branch_ret[l]
        mixed = jax.nn.sigmoid(gate_a) * y_a + jax.nn.sigmoid(gate_r) * y_r
        x = x + mixed @ w_out[l]
        h = rms_norm(x, norm_ffn_g[l])
        u = centred_depthwise_conv(h @ w_up[l], ffn_conv_w[l])
        gate, val = jnp.split(u, 2, axis=-1)
        x = x + (jax.nn.gelu(gate) * val) @ w_down[l]
    return rms_norm(x, norm_final_g)


def setup_inputs(seed: int = 0) -> dict:
    key = jax.random.key(seed)
    ks = jax.random.split(key, 16)
    f32 = jnp.float32

    def nrm(k, shape, scale):
        return jax.random.normal(k, shape, f32) * scale

    base_decay = jnp.log(2.0 ** (5.0 + jnp.arange(RET_HEADS, dtype=f32)) - 1.0)
    return {
        "x_prompt": nrm(ks[0], (BATCH, SEQ, D_MODEL), 1.0),
        "x_sample": nrm(ks[1], (DEC_BATCH, DEC_SEQ, D_MODEL), 1.0),
        "norm_mix_g": 1.0 + nrm(ks[2], (DEPTH, D_MODEL), 0.02),
        "w_in": nrm(ks[3], (DEPTH, D_MODEL, D_IN), D_MODEL ** -0.5),
        "na_rel_bias": nrm(ks[4], (DEPTH, NA_HEADS, 2 * NA_WH - 1, 2 * NA_WW - 1), 0.1),
        "ret_decay_fwd": base_decay[None, :] + nrm(ks[5], (DEPTH, RET_HEADS), 0.1),
        "ret_decay_bwd": base_decay[None, :] + nrm(ks[6], (DEPTH, RET_HEADS), 0.1),
        "ret_norm_g": 1.0 + nrm(ks[7], (DEPTH, RET_V_WIDTH), 0.02),
        "w_branch_attn": nrm(ks[8], (DEPTH, NA_WIDTH, D_MODEL), NA_WIDTH ** -0.5),
        "w_branch_ret": nrm(ks[9], (DEPTH, RET_V_WIDTH, D_MODEL), RET_V_WIDTH ** -0.5),
        "w_out": nrm(ks[10], (DEPTH, D_MODEL, D_MODEL), D_MODEL ** -0.5),
        "norm_ffn_g": 1.0 + nrm(ks[11], (DEPTH, D_MODEL), 0.02),
        "w_up": nrm(ks[12], (DEPTH, D_MODEL, 2 * D_FF), D_MODEL ** -0.5),
        "ffn_conv_w": nrm(ks[13], (DEPTH, CONV_WIDTH, 2 * D_FF), CONV_WIDTH ** -0.5),
        "w_down": nrm(ks[14], (DEPTH, D_FF, D_MODEL), D_FF ** -0.5),
        "norm_final_g": 1.0 + nrm(ks[15], (D_MODEL,), 0.02),
    }


def reference(x_prompt, x_sample, norm_mix_g, w_in, na_rel_bias, ret_decay_fwd, ret_decay_bwd,
              ret_norm_g, w_branch_attn, w_branch_ret, w_out, norm_ffn_g, w_up, ffn_conv_w,
              w_down, norm_final_g):
    y_prompt = trunk(x_prompt, norm_mix_g, w_in, na_rel_bias, ret_decay_fwd, ret_decay_bwd, ret_norm_g,
                     w_branch_attn, w_branch_ret, w_out, norm_ffn_g, w_up, ffn_conv_w, w_down, norm_final_g)
    y_sample = trunk(x_sample, norm_mix_g, w_in, na_rel_bias, ret_decay_fwd, ret_decay_bwd, ret_norm_g,
                     w_branch_attn, w_branch_ret, w_out, norm_ffn_g, w_up, ffn_conv_w, w_down, norm_final_g)
    return (y_prompt, y_sample)
```

```python
import functools

import numpy as np
import jax
import jax.numpy as jnp
from jax import lax
from jax.experimental import pallas as pl
from jax.experimental.pallas import tpu as pltpu

F32 = jnp.float32
BF16 = jnp.bfloat16

D_MODEL = 1024
GRID_W = 64
NA_HEADS = 8
NA_HEAD_DIM = 64
NA_WIDTH = NA_HEADS * NA_HEAD_DIM
NA_WH = 8
NA_WW = 16
RET_HEADS = 4
RET_QK_DIM = 128
RET_V_DIM = 256
RET_QK_WIDTH = RET_HEADS * RET_QK_DIM
RET_V_WIDTH = RET_HEADS * RET_V_DIM
RET_CHUNK = 128
ROPE_BASE = 10000.0
D_FF = 2816
EPS = 1e-6
D_IN = 3 * NA_WIDTH + 2 * RET_QK_WIDTH + 2 * RET_V_WIDTH + 2 * D_MODEL

LANES = 128
SUBLANES = 8
PROJ_TN = 512
FFN_TN = D_FF // 2
MASK_VALUE = -1e30
VMEM_LIMIT = 56 * 1024 * 1024


def _cparams(sem):
    return pltpu.CompilerParams(dimension_semantics=sem, vmem_limit_bytes=VMEM_LIMIT)


def _local(u, u1, n1, n2):
    in1 = u < u1
    loc = jnp.where(in1, lax.rem(u, n1), lax.rem(jnp.maximum(u - u1, 0), n2))
    return loc, jnp.where(in1, n1, n2)


def _rms(x, g):
    ms = jnp.mean(x * x, axis=-1, keepdims=True)
    return (x * lax.rsqrt(ms + EPS)) * g


def _sigmoid(x):
    return 1.0 / (1.0 + jnp.exp(-x))


def _gelu_tanh(x):
    c = np.float32(np.sqrt(2.0 / np.pi))
    return x * (0.5 * (1.0 + jnp.tanh(c * (x + np.float32(0.044715) * (x * x * x)))))


def _inproj_kernel(x_ref, g_ref, w_ref, cos_ref, sin_ref, na_ref, rqk_ref, rv_ref, gts_ref, h_ref):
    j = pl.program_id(1)

    @pl.when(j == 0)
    def _():
        h_ref[...] = _rms(x_ref[...], g_ref[...]).astype(BF16)

    acc = jnp.dot(h_ref[...], w_ref[...], preferred_element_type=F32)

    def rope(scale):
        cos2 = cos_ref[...]
        sin2 = sin_ref[...]
        for hd in range(PROJ_TN // RET_QK_DIM):
            xh = acc[:, hd * RET_QK_DIM:(hd + 1) * RET_QK_DIM]
            r = xh * cos2 + pltpu.roll(xh, RET_QK_DIM // 2, axis=1) * sin2
            if scale is not None:
                r = r * scale
            rqk_ref[:, hd * RET_QK_DIM:(hd + 1) * RET_QK_DIM] = r

    @pl.when(j < 3)
    def _():
        na_ref[...] = acc.astype(BF16)

    @pl.when(j == 3)
    def _():
        rope(None)

    @pl.when(j == 4)
    def _():
        rope(np.float32(RET_QK_DIM ** -0.5))

    @pl.when(jnp.logical_and(j >= 5, j < 7))
    def _():
        rv_ref[...] = acc.astype(BF16)

    @pl.when(j >= 7)
    def _():
        gts_ref[...] = acc


def _inproj(x, g, w, cos2, sin2, tm, u1, n1, n2):
    T = x.shape[0]
    nj = D_IN // PROJ_TN

    def tab_map(i, j):
        return (_local(i, u1, n1, n2)[0], 0)

    return pl.pallas_call(
        _inproj_kernel,
        grid=(T // tm, nj),
        in_specs=[
            pl.BlockSpec((tm, D_MODEL), lambda i, j: (i, 0)),
            pl.BlockSpec((1, D_MODEL), lambda i, j: (0, 0)),
            pl.BlockSpec((D_MODEL, PROJ_TN), lambda i, j: (0, j)),
            pl.BlockSpec((tm, RET_QK_DIM), tab_map),
            pl.BlockSpec((tm, RET_QK_DIM), tab_map),
        ],
        out_specs=[
            pl.BlockSpec((tm, PROJ_TN), lambda i, j: (i, jnp.minimum(j, 2))),
            pl.BlockSpec((tm, PROJ_TN), lambda i, j: (i, jnp.clip(j - 3, 0, 1))),
            pl.BlockSpec((tm, PROJ_TN), lambda i, j: (i, jnp.clip(j - 5, 0, 1))),
            pl.BlockSpec((tm, PROJ_TN), lambda i, j: (i, jnp.clip(j - 7, 0, 5))),
        ],
        out_shape=[
            jax.ShapeDtypeStruct((T, 3 * NA_WIDTH), BF16),
            jax.ShapeDtypeStruct((T, 2 * RET_QK_WIDTH), F32),
            jax.ShapeDtypeStruct((T, RET_V_WIDTH), BF16),
            jax.ShapeDtypeStruct((T, RET_V_WIDTH + 2 * D_MODEL), F32),
        ],
        scratch_shapes=[pltpu.VMEM((tm, D_MODEL), BF16)],
        compiler_params=_cparams(("parallel", "arbitrary")),
        name="inproj",
    )(x, g, w, cos2, sin2)


def _na_kernel(q_ref, k_ref, v_ref, tab_ref, *rest, rows):
    o_ref = rest[-1]
    band = NA_WH * GRID_W
    lane = lax.broadcasted_iota(jnp.int32, (GRID_W, LANES), 1)
    low = lane < NA_HEAD_DIM
    scale = np.float32(NA_HEAD_DIM ** -0.5)

    def body(r, carry):
        rs = jnp.clip(r - NA_WH // 2, 0, rows - NA_WH)
        d = r - rs
        q = q_ref[pl.ds(pl.multiple_of(r * GRID_W, GRID_W), GRID_W), :]
        kb = k_ref[pl.ds(pl.multiple_of(rs * GRID_W, GRID_W), band), :]
        vb = v_ref[pl.ds(pl.multiple_of(rs * GRID_W, GRID_W), band), :]
        outs = []
        for hh in range(2):
            qm = jnp.where(low if hh == 0 else jnp.logical_not(low), q, jnp.zeros_like(q))
            s = lax.dot_general(qm, kb, (((1,), (1,)), ((), ())), preferred_element_type=F32)
            s = s * scale + tab_ref[hh, d]
            m = jnp.max(s, axis=-1, keepdims=True)
            p = jnp.exp(s - m)
            l = jnp.sum(p, axis=-1, keepdims=True)
            o = jnp.dot(p.astype(BF16), vb, preferred_element_type=F32)
            outs.append(o / l)
        o_ref[pl.ds(pl.multiple_of(r * GRID_W, GRID_W), GRID_W), :] = (
            jnp.where(low, outs[0], outs[1]).astype(BF16))
        return carry

    lax.fori_loop(0, rows, body, 0)


def _na(na, tab, prev, tok0, nseq, S, T):
    rows = S // GRID_W
    b0 = tok0 // S
    npair = NA_HEADS // 2
    in_specs = [
        pl.BlockSpec((S, LANES), lambda b, hp: (b0 + b, hp)),
        pl.BlockSpec((S, LANES), lambda b, hp: (b0 + b, npair + hp)),
        pl.BlockSpec((S, LANES), lambda b, hp: (b0 + b, 2 * npair + hp)),
        pl.BlockSpec((2, NA_WH, GRID_W, NA_WH * GRID_W), lambda b, hp: (hp, 0, 0, 0)),
    ]
    args = [na, na, na, tab]
    aliases = {}
    if prev is not None:
        in_specs.append(pl.BlockSpec(memory_space=pl.ANY))
        args.append(prev)
        aliases = {4: 0}
    return pl.pallas_call(
        functools.partial(_na_kernel, rows=rows),
        grid=(nseq, npair),
        in_specs=in_specs,
        out_specs=pl.BlockSpec((S, LANES), lambda b, hp: (b0 + b, hp)),
        out_shape=jax.ShapeDtypeStruct((T, NA_WIDTH), BF16),
        input_output_aliases=aliases,
        compiler_params=_cparams(("parallel", "parallel")),
        name="na",
    )(*args)


def _na_bias_table(rel_bias):
    col = np.arange(GRID_W)
    col_start = np.clip(col - NA_WW // 2, 0, GRID_W - NA_WW)
    kc = np.arange(GRID_W)
    valid = (kc[None, :] >= col_start[:, None]) & (kc[None, :] < col_start[:, None] + NA_WW)
    cbi = np.clip(kc[None, :] - col[:, None] + (NA_WW - 1), 0, 2 * NA_WW - 2)
    d = np.arange(NA_WH)
    w = np.arange(NA_WH)
    rbi = w[None, :] - d[:, None] + (NA_WH - 1)
    t = rel_bias[:, rbi[:, None, :, None], cbi[None, :, None, :]]
    t = jnp.where(valid[None, None, :, None, :], t.astype(F32), MASK_VALUE)
    return t.reshape(rel_bias.shape[0], NA_WH, GRID_W, NA_WH * GRID_W)


def _ret_bwd_kernel(cd_ref, k_ref, v_ref, kdb_ref, sb_ref, st_ref, *, u1, n1, n2):
    c = pl.num_programs(0) - 1 - pl.program_id(0)
    loc, n = _local(c, u1, n1, n2)

    @pl.when(loc == n - 1)
    def _():
        st_ref[...] = jnp.zeros_like(st_ref)

    for h in range(RET_HEADS):
        st = st_ref[h]
        sb_ref[0, h] = st.astype(BF16)
        kd = (k_ref[:, h * RET_QK_DIM:(h + 1) * RET_QK_DIM] * kdb_ref[h]).astype(BF16)
        vh = v_ref[:, h * RET_V_DIM:(h + 1) * RET_V_DIM]
        st_ref[h] = st * cd_ref[1, h] + lax.dot_general(
            kd, vh, (((0,), (0,)), ((), ())), preferred_element_type=F32)


def _ret_fwd_kernel(cd_ref, q_ref, k_ref, v_ref, sb_ref, rg_ref, gn_ref, dm_ref, qdf_ref, qdb_ref,
                    kdf_ref, o_ref, st_ref, *, u1, n1, n2):
    c = pl.program_id(0)
    loc, _ = _local(c, u1, n1, n2)

    @pl.when(loc == 0)
    def _():
        st_ref[...] = jnp.zeros_like(st_ref)

    for h in range(RET_HEADS):
        qh = q_ref[:, h * RET_QK_DIM:(h + 1) * RET_QK_DIM]
        kh = k_ref[:, h * RET_QK_DIM:(h + 1) * RET_QK_DIM]
        vh = v_ref[:, h * RET_V_DIM:(h + 1) * RET_V_DIM]
        st = st_ref[h]
        s = lax.dot_general(qh.astype(BF16), kh.astype(BF16), (((1,), (1,)), ((), ())),
                            preferred_element_type=F32)
        o = jnp.dot((s * dm_ref[h]).astype(BF16), vh, preferred_element_type=F32)
        o = o + jnp.dot((qh * qdf_ref[h]).astype(BF16), st.astype(BF16), preferred_element_type=F32)
        o = o + jnp.dot((qh * qdb_ref[h]).astype(BF16), sb_ref[0, h], preferred_element_type=F32)
        st_ref[h] = st * cd_ref[0, h] + lax.dot_general(
            (kh * kdf_ref[h]).astype(BF16), vh, (((0,), (0,)), ((), ())), preferred_element_type=F32)
        mu = jnp.mean(o, axis=-1, keepdims=True)
        oc = o - mu
        var = jnp.mean(oc * oc, axis=-1, keepdims=True)
        y = (oc * lax.rsqrt(var + EPS)) * gn_ref[:, h * RET_V_DIM:(h + 1) * RET_V_DIM]
        rg = rg_ref[:, h * RET_V_DIM:(h + 1) * RET_V_DIM]
        o_ref[:, h * RET_V_DIM:(h + 1) * RET_V_DIM] = ((rg * _sigmoid(rg)) * y).astype(BF16)


def _retention(rqk, rv, gts, gn, decay_f, decay_b, u1, n1, n2):
    T = rqk.shape[0]
    C = RET_CHUNK
    nc = T // C
    lgf = jax.nn.log_sigmoid(decay_f.astype(F32))
    lgb = jax.nn.log_sigmoid(decay_b.astype(F32))
    idx = jnp.arange(C, dtype=F32)
    diff = idx[:, None] - idx[None, :]
    dmat = jnp.where(diff[None] >= 0,
                     jnp.exp(jnp.maximum(diff, 0.0)[None] * lgf[:, None, None]),
                     jnp.exp(jnp.maximum(-diff, 0.0)[None] * lgb[:, None, None]))

    def rows(e):
        return jnp.broadcast_to(e[:, :, None], (RET_HEADS, C, RET_QK_DIM))

    qdf = rows(jnp.exp((idx + 1.0)[None, :] * lgf[:, None]))
    kdf = rows(jnp.exp((C - 1.0 - idx)[None, :] * lgf[:, None]))
    qdb = rows(jnp.exp((C - idx)[None, :] * lgb[:, None]))
    kdb = rows(jnp.exp(idx[None, :] * lgb[:, None]))
    cd = jnp.stack([jnp.exp(C * lgf), jnp.exp(C * lgb)])

    smem = pl.BlockSpec(memory_space=pltpu.SMEM)
    full3 = pl.BlockSpec((RET_HEADS, C, RET_QK_DIM), lambda c: (0, 0, 0))
    state = pltpu.VMEM((RET_HEADS, RET_QK_DIM, RET_V_DIM), F32)

    sb = pl.pallas_call(
        functools.partial(_ret_bwd_kernel, u1=u1, n1=n1, n2=n2),
        grid=(nc,),
        in_specs=[
            smem,
            pl.BlockSpec((C, RET_QK_WIDTH), lambda c: (nc - 1 - c, 1)),
            pl.BlockSpec((C, RET_V_WIDTH), lambda c: (nc - 1 - c, 0)),
            full3,
        ],
        out_specs=pl.BlockSpec((1, RET_HEADS, RET_QK_DIM, RET_V_DIM), lambda c: (nc - 1 - c, 0, 0, 0)),
        out_shape=jax.ShapeDtypeStruct((nc, RET_HEADS, RET_QK_DIM, RET_V_DIM), BF16),
        scratch_shapes=[state],
        compiler_params=_cparams(("arbitrary",)),
        name="ret_bwd",
    )(cd, rqk, rv, kdb)

    return pl.pallas_call(
        functools.partial(_ret_fwd_kernel, u1=u1, n1=n1, n2=n2),
        grid=(nc,),
        in_specs=[
            smem,
            pl.BlockSpec((C, RET_QK_WIDTH), lambda c: (c, 0)),
            pl.BlockSpec((C, RET_QK_WIDTH), lambda c: (c, 1)),
            pl.BlockSpec((C, RET_V_WIDTH), lambda c: (c, 0)),
            pl.BlockSpec((1, RET_HEADS, RET_QK_DIM, RET_V_DIM), lambda c: (c, 0, 0, 0)),
            pl.BlockSpec((C, RET_V_WIDTH), lambda c: (c, 0)),
            pl.BlockSpec((1, RET_V_WIDTH), lambda c: (0, 0)),
            pl.BlockSpec((RET_HEADS, C, C), lambda c: (0, 0, 0)),
            full3, full3, full3,
        ],
        out_specs=pl.BlockSpec((C, RET_V_WIDTH), lambda c: (c, 0)),
        out_shape=jax.ShapeDtypeStruct((T, RET_V_WIDTH), BF16),
        scratch_shapes=[state],
        compiler_params=_cparams(("arbitrary",)),
        name="ret_fwd",
    )(cd, rqk, rqk, rv, sb, gts, gn, dmat, qdf, qdb, kdf)


def _merge_kernel(a_ref, ro_ref, ga_ref, gr_ref, x_ref, wa_ref, wr_ref, wo_ref, o_ref):
    ya = jnp.dot(a_ref[...], wa_ref[...], preferred_element_type=F32)
    yr = jnp.dot(ro_ref[...], wr_ref[...], preferred_element_type=F32)
    mixed = _sigmoid(ga_ref[...]) * ya + _sigmoid(gr_ref[...]) * yr
    o_ref[...] = x_ref[...] + jnp.dot(mixed.astype(BF16), wo_ref[...], preferred_element_type=F32)


def _merge(a, ro, gts, x, wa, wr, wo, tm):
    T = x.shape[0]
    return pl.pallas_call(
        _merge_kernel,
        grid=(T // tm,),
        in_specs=[
            pl.BlockSpec((tm, NA_WIDTH), lambda i: (i, 0)),
            pl.BlockSpec((tm, RET_V_WIDTH), lambda i: (i, 0)),
            pl.BlockSpec((tm, D_MODEL), lambda i: (i, 1)),
            pl.BlockSpec((tm, D_MODEL), lambda i: (i, 2)),
            pl.BlockSpec((tm, D_MODEL), lambda i: (i, 0)),
            pl.BlockSpec((NA_WIDTH, D_MODEL), lambda i: (0, 0)),
            pl.BlockSpec((RET_V_WIDTH, D_MODEL), lambda i: (0, 0)),
            pl.BlockSpec((D_MODEL, D_MODEL), lambda i: (0, 0)),
        ],
        out_specs=pl.BlockSpec((tm, D_MODEL), lambda i: (i, 0)),
        out_shape=jax.ShapeDtypeStruct((T, D_MODEL), F32),
        compiler_params=_cparams(("parallel",)),
        name="merge",
    )(a, ro, gts, gts, x, wa, wr, wo)


def _ffn_kernel(xp_ref, x_ref, xn_ref, g_ref, wg_ref, wv_ref, cg_ref, cv_ref, wd_ref, gf_ref,
                o_ref, h_ref, acc_ref, *, tm, u1, n1, n2, final):
    i = pl.program_id(0)
    j = pl.program_id(1)
    halo = SUBLANES

    @pl.when(j == 0)
    def _():
        loc, n = _local(i, u1, n1, n2)
        g = g_ref[...]
        hp = _rms(xp_ref[...], g)
        hn = _rms(xn_ref[...], g)
        h_ref[0:halo, :] = jnp.where(loc == 0, 0.0, hp).astype(BF16)
        h_ref[halo:halo + tm, :] = _rms(x_ref[...], g).astype(BF16)
        h_ref[halo + tm:2 * halo + tm, :] = jnp.where(loc == n - 1, 0.0, hn).astype(BF16)

    h = h_ref[...]

    def conv(u, c_ref):
        return (u[halo - 1:halo - 1 + tm] * c_ref[0:1, :] + u[halo:halo + tm] * c_ref[1:2, :]
                + u[halo + 1:halo + 1 + tm] * c_ref[2:3, :])

    gate = conv(jnp.dot(h, wg_ref[...], preferred_element_type=F32), cg_ref)
    val = conv(jnp.dot(h, wv_ref[...], preferred_element_type=F32), cv_ref)
    part = jnp.dot((_gelu_tanh(gate) * val).astype(BF16), wd_ref[...], preferred_element_type=F32)

    @pl.when(j == 0)
    def _():
        acc_ref[...] = part

    @pl.when(j == pl.num_programs(1) - 1)
    def _():
        y = x_ref[...] + (acc_ref[...] + part)
        if final:
            y = _rms(y, gf_ref[...])
        o_ref[...] = y


def _ffn(x, g, w_up, conv_w, w_down, gf, tm, u1, n1, n2, final):
    T = x.shape[0]
    nj = D_FF // FFN_TN
    hb = tm // SUBLANES
    last = T // SUBLANES - 1
    return pl.pallas_call(
        functools.partial(_ffn_kernel, tm=tm, u1=u1, n1=n1, n2=n2, final=final),
        grid=(T // tm, nj),
        in_specs=[
            pl.BlockSpec((SUBLANES, D_MODEL), lambda i, j: (jnp.maximum(i * hb - 1, 0), 0)),
            pl.BlockSpec((tm, D_MODEL), lambda i, j: (i, 0)),
            pl.BlockSpec((SUBLANES, D_MODEL), lambda i, j: (jnp.minimum((i + 1) * hb, last), 0)),
            pl.BlockSpec((1, D_MODEL), lambda i, j: (0, 0)),
            pl.BlockSpec((D_MODEL, FFN_TN), lambda i, j: (0, j)),
            pl.BlockSpec((D_MODEL, FFN_TN), lambda i, j: (0, nj + j)),
            pl.BlockSpec((3, FFN_TN), lambda i, j: (0, j)),
            pl.BlockSpec((3, FFN_TN), lambda i, j: (0, nj + j)),
            pl.BlockSpec((FFN_TN, D_MODEL), lambda i, j: (j, 0)),
            pl.BlockSpec((1, D_MODEL), lambda i, j: (0, 0)),
        ],
        out_specs=pl.BlockSpec((tm, D_MODEL), lambda i, j: (i, 0)),
        out_shape=jax.ShapeDtypeStruct((T, D_MODEL), F32),
        scratch_shapes=[pltpu.VMEM((tm + 2 * SUBLANES, D_MODEL), BF16), pltpu.VMEM((tm, D_MODEL), F32)],
        compiler_params=_cparams(("parallel", "arbitrary")),
        name="ffn",
    )(x, x, x, g, w_up, w_up, conv_w, conv_w, w_down, gf)


def _rope_tables(seq_len):
    inv_freq = ROPE_BASE ** (-jnp.arange(0, RET_QK_DIM, 2, dtype=F32) / RET_QK_DIM)
    ang = jnp.arange(seq_len, dtype=F32)[:, None] * inv_freq[None, :]
    cos, sin = jnp.cos(ang), jnp.sin(ang)
    return jnp.concatenate([cos, cos], axis=1), jnp.concatenate([-sin, sin], axis=1)


def _pick_tile(limit, *sizes):
    t = limit
    while any(s % t for s in sizes):
        t //= 2
    return t


def kernel(x_prompt, x_sample, norm_mix_g, w_in, na_rel_bias, ret_decay_fwd, ret_decay_bwd, ret_norm_g,
           w_branch_attn, w_branch_ret, w_out, norm_ffn_g, w_up, ffn_conv_w, w_down, norm_final_g):
    B1, S1, D = x_prompt.shape
    B2, S2, _ = x_sample.shape
    T1, T2 = B1 * S1, B2 * S2
    T = T1 + T2
    depth = w_in.shape[0]
    assert D == D_MODEL and w_in.shape[2] == D_IN
    assert S1 % (NA_WH * GRID_W) == 0 and S2 % (NA_WH * GRID_W) == 0 and T1 % S2 == 0

    x = jnp.concatenate([x_prompt.reshape(T1, D), x_sample.reshape(T2, D)], axis=0)
    cos2, sin2 = _rope_tables(max(S1, S2))

    tm_proj = _pick_tile(1024, S1, S2)
    tm_mix = _pick_tile(512, S1, S2)

    def units(t):
        return T1 // t, S1 // t, S2 // t

    for l in range(depth):
        na, rqk, rv, gts = _inproj(x, norm_mix_g[l][None].astype(F32), w_in[l].astype(BF16),
                                   cos2, sin2, tm_proj, *units(tm_proj))
        tab = _na_bias_table(na_rel_bias[l])
        a = _na(na, tab, None, 0, B1, S1, T)
        a = _na(na, tab, a, T1, B2, S2, T)
        ro = _retention(rqk, rv, gts, ret_norm_g[l][None].astype(F32), ret_decay_fwd[l], ret_decay_bwd[l],
                        *units(RET_CHUNK))
        x = _merge(a, ro, gts, x, w_branch_attn[l].astype(BF16), w_branch_ret[l].astype(BF16),
                   w_out[l].astype(BF16), tm_mix)
        x = _ffn(x, norm_ffn_g[l][None].astype(F32), w_up[l].astype(BF16), ffn_conv_w[l].astype(F32),
                 w_down[l].astype(BF16), norm_final_g[None].astype(F32), tm_mix, *units(tm_mix),
                 final=(l == depth - 1))

    return x[:T1].reshape(B1, S1, D), x[T1:].reshape(B2, S2, D)
```

```python
import functools

import numpy as np
import jax
import jax.numpy as jnp
from jax import lax
from jax.experimental import pallas as pl
from jax.experimental.pallas import tpu as pltpu

F32 = jnp.float32
BF16 = jnp.bfloat16

D_MODEL = 1024
GRID_W = 64
NA_HEADS = 8
NA_HEAD_DIM = 64
NA_WIDTH = NA_HEADS * NA_HEAD_DIM
NA_WH = 8
NA_WW = 16
RET_HEADS = 4
RET_QK_DIM = 128
RET_V_DIM = 256
RET_QK_WIDTH = RET_HEADS * RET_QK_DIM
RET_V_WIDTH = RET_HEADS * RET_V_DIM
RET_CHUNK = 128
ROPE_BASE = 10000.0
D_FF = 2816
EPS = 1e-6
D_IN = 3 * NA_WIDTH + 2 * RET_QK_WIDTH + 2 * RET_V_WIDTH + 2 * D_MODEL

LANES = 128
SUBLANES = 8
PROJ_TN = 512
FFN_TN = 256
RET_CHUNKS_PER_STEP = 4
NA_ROW_GROUP = 8
MASK_VALUE = -1e30
VMEM_LIMIT = 56 * 1024 * 1024


def _cparams(sem):
    return pltpu.CompilerParams(dimension_semantics=sem, vmem_limit_bytes=VMEM_LIMIT)


def _local(u, u1, n1, n2):
    in1 = u < u1
    loc = jnp.where(in1, lax.rem(u, n1), lax.rem(jnp.maximum(u - u1, 0), n2))
    return loc, jnp.where(in1, n1, n2)


def _rms(x, g):
    ms = jnp.mean(x * x, axis=-1, keepdims=True)
    return (x * lax.rsqrt(ms + EPS)) * g


def _sigmoid(x):
    return 1.0 / (1.0 + jnp.exp(-x))


def _gelu_tanh(x):
    c = np.float32(np.sqrt(2.0 / np.pi))
    return x * (0.5 * (1.0 + jnp.tanh(c * (x + np.float32(0.044715) * (x * x * x)))))


def _inproj_kernel(x_ref, g_ref, w_ref, cos_ref, sin_ref, na_ref, rqk_ref, rv_ref, gts_ref, h_ref):
    h_ref[...] = _rms(x_ref[...], g_ref[...]).astype(BF16)

    def proj(c0):
        return jnp.dot(h_ref[...], w_ref[:, c0:c0 + PROJ_TN], preferred_element_type=F32)

    def rope(acc, c0, scale):
        cos2 = cos_ref[...]
        sin2 = sin_ref[...]
        for hd in range(PROJ_TN // RET_QK_DIM):
            xh = acc[:, hd * RET_QK_DIM:(hd + 1) * RET_QK_DIM]
            r = xh * cos2 + pltpu.roll(xh, RET_QK_DIM // 2, axis=1) * sin2
            if scale is not None:
                r = r * scale
            rqk_ref[:, c0 + hd * RET_QK_DIM:c0 + (hd + 1) * RET_QK_DIM] = r

    col = 0
    for c0 in range(0, 3 * NA_WIDTH, PROJ_TN):
        na_ref[:, c0:c0 + PROJ_TN] = proj(col + c0).astype(BF16)
    col += 3 * NA_WIDTH
    rope(proj(col), 0, None)
    rope(proj(col + RET_QK_WIDTH), RET_QK_WIDTH, np.float32(RET_QK_DIM ** -0.5))
    col += 2 * RET_QK_WIDTH
    for c0 in range(0, RET_V_WIDTH, PROJ_TN):
        rv_ref[:, c0:c0 + PROJ_TN] = proj(col + c0).astype(BF16)
    col += RET_V_WIDTH
    for c0 in range(0, RET_V_WIDTH + 2 * D_MODEL, PROJ_TN):
        gts_ref[:, c0:c0 + PROJ_TN] = proj(col + c0)


def _inproj(x, g, w, cos2, sin2, tm, u1, n1, n2):
    T = x.shape[0]
    assert RET_QK_WIDTH == PROJ_TN

    def tab_map(i):
        return (_local(i, u1, n1, n2)[0], 0)

    return pl.pallas_call(
        _inproj_kernel,
        grid=(T // tm,),
        in_specs=[
            pl.BlockSpec((tm, D_MODEL), lambda i: (i, 0)),
            pl.BlockSpec((1, D_MODEL), lambda i: (0, 0)),
            pl.BlockSpec((D_MODEL, D_IN), lambda i: (0, 0), pipeline_mode=pl.Buffered(1)),
            pl.BlockSpec((tm, RET_QK_DIM), tab_map),
            pl.BlockSpec((tm, RET_QK_DIM), tab_map),
        ],
        out_specs=[
            pl.BlockSpec((tm, 3 * NA_WIDTH), lambda i: (i, 0)),
            pl.BlockSpec((tm, 2 * RET_QK_WIDTH), lambda i: (i, 0)),
            pl.BlockSpec((tm, RET_V_WIDTH), lambda i: (i, 0)),
            pl.BlockSpec((tm, RET_V_WIDTH + 2 * D_MODEL), lambda i: (i, 0)),
        ],
        out_shape=[
            jax.ShapeDtypeStruct((T, 3 * NA_WIDTH), BF16),
            jax.ShapeDtypeStruct((T, 2 * RET_QK_WIDTH), F32),
            jax.ShapeDtypeStruct((T, RET_V_WIDTH), BF16),
            jax.ShapeDtypeStruct((T, RET_V_WIDTH + 2 * D_MODEL), F32),
        ],
        scratch_shapes=[pltpu.VMEM((tm, D_MODEL), BF16)],
        compiler_params=_cparams(("parallel",)),
        name="inproj",
    )(x, g, w, cos2, sin2)


def _na_kernel(q_ref, k_ref, v_ref, tab_ref, *rest, rows):
    o_ref = rest[-1]
    band = NA_WH * GRID_W
    pair = 2 * GRID_W
    lane = lax.broadcasted_iota(jnp.int32, (pair, LANES), 1)
    sub = lax.broadcasted_iota(jnp.int32, (pair, LANES), 0)
    keep = (lane < NA_HEAD_DIM) == (sub < GRID_W)
    low = lax.broadcasted_iota(jnp.int32, (GRID_W, LANES), 1) < NA_HEAD_DIM
    scale = NA_HEAD_DIM ** -0.5

    def tok(row):
        return pl.ds(pl.multiple_of(row * GRID_W, GRID_W), GRID_W)

    def scores(r):
        rs = jnp.clip(r - NA_WH // 2, 0, rows - NA_WH)
        q = q_ref[tok(r), :] * scale
        kb = k_ref[pl.ds(pl.multiple_of(rs * GRID_W, GRID_W), band), :]
        qm = jnp.where(keep, jnp.concatenate([q, q], axis=0), jnp.zeros((pair, LANES), BF16))
        s = lax.dot_general(qm, kb, (((1,), (1,)), ((), ())), preferred_element_type=F32)
        return s + tab_ref[0, r - rs]

    def softmax(s):
        p = jnp.exp(s - jnp.max(s, axis=-1, keepdims=True))
        return p.astype(BF16), jnp.sum(p, axis=-1, keepdims=True)

    def weighted(r, p, l):
        rs = jnp.clip(r - NA_WH // 2, 0, rows - NA_WH)
        vb = v_ref[pl.ds(pl.multiple_of(rs * GRID_W, GRID_W), band), :]
        o = jnp.dot(p, vb, preferred_element_type=F32) / l
        o_ref[tok(r), :] = jnp.where(low, o[:GRID_W], o[GRID_W:]).astype(BF16)

    def body(g, carry):
        rs_ = [g * NA_ROW_GROUP + i for i in range(NA_ROW_GROUP)]
        ss = [scores(r) for r in rs_]
        pls = [softmax(s) for s in ss]
        for r, (p, l) in zip(rs_, pls):
            weighted(r, p, l)
        return carry

    lax.fori_loop(0, rows // NA_ROW_GROUP, body, 0)


def _na(na, tab, prev, tok0, nseq, S, T):
    rows = S // GRID_W
    b0 = tok0 // S
    npair = NA_HEADS // 2
    in_specs = [
        pl.BlockSpec((S, LANES), lambda b, hp: (b0 + b, hp)),
        pl.BlockSpec((S, LANES), lambda b, hp: (b0 + b, npair + hp)),
        pl.BlockSpec((S, LANES), lambda b, hp: (b0 + b, 2 * npair + hp)),
        pl.BlockSpec((1, NA_WH, 2 * GRID_W, NA_WH * GRID_W), lambda b, hp: (hp, 0, 0, 0)),
    ]
    args = [na, na, na, tab]
    aliases = {}
    if prev is not None:
        in_specs.append(pl.BlockSpec(memory_space=pl.ANY))
        args.append(prev)
        aliases = {4: 0}
    return pl.pallas_call(
        functools.partial(_na_kernel, rows=rows),
        grid=(nseq, npair),
        in_specs=in_specs,
        out_specs=pl.BlockSpec((S, LANES), lambda b, hp: (b0 + b, hp)),
        out_shape=jax.ShapeDtypeStruct((T, NA_WIDTH), BF16),
        input_output_aliases=aliases,
        compiler_params=_cparams(("parallel", "parallel")),
        name="na",
    )(*args)


def _na_bias_table(rel_bias):
    H = rel_bias.shape[0]
    col = np.arange(GRID_W)
    col_start = np.clip(col - NA_WW // 2, 0, GRID_W - NA_WW)
    valid = (col[None, :] >= col_start[:, None]) & (col[None, :] < col_start[:, None] + NA_WW)
    rel = col[None, :] - col[:, None] + (NA_WW - 1)
    pick = (valid[:, :, None] & (rel[:, :, None] == np.arange(2 * NA_WW - 1))).astype(np.float32)
    cm = jnp.einsum('hij,ckj->hick', rel_bias.astype(F32), pick, precision=lax.Precision.HIGHEST)
    cm = jnp.where(valid[None, None], cm, MASK_VALUE)
    per_d = [jnp.transpose(cm[:, NA_WH - 1 - d:2 * NA_WH - 1 - d], (0, 2, 1, 3)) for d in range(NA_WH)]
    t = jnp.stack(per_d, axis=1).reshape(H // 2, 2, NA_WH, GRID_W, NA_WH * GRID_W)
    return jnp.transpose(t, (0, 2, 1, 3, 4)).reshape(H // 2, NA_WH, 2 * GRID_W, NA_WH * GRID_W)


def _ret_bwd_kernel(cd_ref, k_ref, v_ref, kdb_ref, sb_ref, st_ref, *, u1, n1, n2):
    step = pl.num_programs(0) - 1 - pl.program_id(0)
    loc, n = _local(step, u1, n1, n2)

    @pl.when(loc == n - 1)
    def _():
        st_ref[...] = jnp.zeros_like(st_ref)

    for cc in reversed(range(RET_CHUNKS_PER_STEP)):
        tok = slice(cc * RET_CHUNK, (cc + 1) * RET_CHUNK)
        for h in range(RET_HEADS):
            st = st_ref[h]
            sb_ref[cc, h] = st.astype(BF16)
            kd = (k_ref[tok, h * RET_QK_DIM:(h + 1) * RET_QK_DIM] * kdb_ref[h]).astype(BF16)
            vh = v_ref[tok, h * RET_V_DIM:(h + 1) * RET_V_DIM]
            st_ref[h] = st * cd_ref[1, h] + lax.dot_general(
                kd, vh, (((0,), (0,)), ((), ())), preferred_element_type=F32)


def _ret_fwd_kernel(cd_ref, q_ref, k_ref, v_ref, sb_ref, rg_ref, gn_ref, dm_ref, qdf_ref, qdb_ref,
                    kdf_ref, o_ref, st_ref, *, u1, n1, n2):
    loc, _ = _local(pl.program_id(0), u1, n1, n2)

    @pl.when(loc == 0)
    def _():
        st_ref[...] = jnp.zeros_like(st_ref)

    for cc in range(RET_CHUNKS_PER_STEP):
        tok = slice(cc * RET_CHUNK, (cc + 1) * RET_CHUNK)
        for h in range(RET_HEADS):
            qh = q_ref[tok, h * RET_QK_DIM:(h + 1) * RET_QK_DIM]
            kh = k_ref[tok, h * RET_QK_DIM:(h + 1) * RET_QK_DIM]
            vh = v_ref[tok, h * RET_V_DIM:(h + 1) * RET_V_DIM]
            st = st_ref[h]
            s = lax.dot_general(qh.astype(BF16), kh.astype(BF16), (((1,), (1,)), ((), ())),
                                preferred_element_type=F32)
            o = jnp.dot((s * dm_ref[h]).astype(BF16), vh, preferred_element_type=F32)
            o = o + jnp.dot((qh * qdf_ref[h]).astype(BF16), st.astype(BF16), preferred_element_type=F32)
            o = o + jnp.dot((qh * qdb_ref[h]).astype(BF16), sb_ref[cc, h], preferred_element_type=F32)
            st_ref[h] = st * cd_ref[0, h] + lax.dot_general(
                (kh * kdf_ref[h]).astype(BF16), vh, (((0,), (0,)), ((), ())), preferred_element_type=F32)
            mu = jnp.mean(o, axis=-1, keepdims=True)
            oc = o - mu
            var = jnp.mean(oc * oc, axis=-1, keepdims=True)
            y = (oc * lax.rsqrt(var + EPS)) * gn_ref[:, h * RET_V_DIM:(h + 1) * RET_V_DIM]
            rg = rg_ref[tok, h * RET_V_DIM:(h + 1) * RET_V_DIM]
            o_ref[tok, h * RET_V_DIM:(h + 1) * RET_V_DIM] = ((rg * _sigmoid(rg)) * y).astype(BF16)


def _retention(rqk, rv, gts, gn, decay_f, decay_b, u1, n1, n2):
    T = rqk.shape[0]
    C = RET_CHUNK
    cps = RET_CHUNKS_PER_STEP
    nc = T // (C * cps)
    lgf = jax.nn.log_sigmoid(decay_f.astype(F32))
    lgb = jax.nn.log_sigmoid(decay_b.astype(F32))
    idx = jnp.arange(C, dtype=F32)
    diff = idx[:, None] - idx[None, :]
    dmat = jnp.where(diff[None] >= 0,
                     jnp.exp(jnp.maximum(diff, 0.0)[None] * lgf[:, None, None]),
                     jnp.exp(jnp.maximum(-diff, 0.0)[None] * lgb[:, None, None]))

    def rows(e):
        return jnp.broadcast_to(e[:, :, None], (RET_HEADS, C, RET_QK_DIM))

    qdf = rows(jnp.exp((idx + 1.0)[None, :] * lgf[:, None]))
    kdf = rows(jnp.exp((C - 1.0 - idx)[None, :] * lgf[:, None]))
    qdb = rows(jnp.exp((C - idx)[None, :] * lgb[:, None]))
    kdb = rows(jnp.exp(idx[None, :] * lgb[:, None]))
    cd = jnp.stack([jnp.exp(C * lgf), jnp.exp(C * lgb)])

    smem = pl.BlockSpec(memory_space=pltpu.SMEM)
    full3 = pl.BlockSpec((RET_HEADS, C, RET_QK_DIM), lambda c: (0, 0, 0))
    state = pltpu.VMEM((RET_HEADS, RET_QK_DIM, RET_V_DIM), F32)

    sb = pl.pallas_call(
        functools.partial(_ret_bwd_kernel, u1=u1, n1=n1, n2=n2),
        grid=(nc,),
        in_specs=[
            smem,
            pl.BlockSpec((cps * C, RET_QK_WIDTH), lambda c: (nc - 1 - c, 1)),
            pl.BlockSpec((cps * C, RET_V_WIDTH), lambda c: (nc - 1 - c, 0)),
            full3,
        ],
        out_specs=pl.BlockSpec((cps, RET_HEADS, RET_QK_DIM, RET_V_DIM), lambda c: (nc - 1 - c, 0, 0, 0)),
        out_shape=jax.ShapeDtypeStruct((nc * cps, RET_HEADS, RET_QK_DIM, RET_V_DIM), BF16),
        scratch_shapes=[state],
        compiler_params=_cparams(("arbitrary",)),
        name="ret_bwd",
    )(cd, rqk, rv, kdb)

    return pl.pallas_call(
        functools.partial(_ret_fwd_kernel, u1=u1, n1=n1, n2=n2),
        grid=(nc,),
        in_specs=[
            smem,
            pl.BlockSpec((cps * C, RET_QK_WIDTH), lambda c: (c, 0)),
            pl.BlockSpec((cps * C, RET_QK_WIDTH), lambda c: (c, 1)),
            pl.BlockSpec((cps * C, RET_V_WIDTH), lambda c: (c, 0)),
            pl.BlockSpec((cps, RET_HEADS, RET_QK_DIM, RET_V_DIM), lambda c: (c, 0, 0, 0)),
            pl.BlockSpec((cps * C, RET_V_WIDTH), lambda c: (c, 0)),
            pl.BlockSpec((1, RET_V_WIDTH), lambda c: (0, 0)),
            pl.BlockSpec((RET_HEADS, C, C), lambda c: (0, 0, 0)),
            full3, full3, full3,
        ],
        out_specs=pl.BlockSpec((cps * C, RET_V_WIDTH), lambda c: (c, 0)),
        out_shape=jax.ShapeDtypeStruct((T, RET_V_WIDTH), BF16),
        scratch_shapes=[state],
        compiler_params=_cparams(("arbitrary",)),
        name="ret_fwd",
    )(cd, rqk, rqk, rv, sb, gts, gn, dmat, qdf, qdb, kdf)


def _merge_kernel(a_ref, ro_ref, ga_ref, gr_ref, x_ref, wa_ref, wr_ref, wo_ref, o_ref):
    ya = jnp.dot(a_ref[...], wa_ref[...], preferred_element_type=F32)
    yr = jnp.dot(ro_ref[...], wr_ref[...], preferred_element_type=F32)
    mixed = _sigmoid(ga_ref[...]) * ya + _sigmoid(gr_ref[...]) * yr
    o_ref[...] = x_ref[...] + jnp.dot(mixed.astype(BF16), wo_ref[...], preferred_element_type=F32)


def _merge(a, ro, gts, x, wa, wr, wo, tm):
    T = x.shape[0]
    return pl.pallas_call(
        _merge_kernel,
        grid=(T // tm,),
        in_specs=[
            pl.BlockSpec((tm, NA_WIDTH), lambda i: (i, 0)),
            pl.BlockSpec((tm, RET_V_WIDTH), lambda i: (i, 0)),
            pl.BlockSpec((tm, D_MODEL), lambda i: (i, 1)),
            pl.BlockSpec((tm, D_MODEL), lambda i: (i, 2)),
            pl.BlockSpec((tm, D_MODEL), lambda i: (i, 0)),
            pl.BlockSpec((NA_WIDTH, D_MODEL), lambda i: (0, 0)),
            pl.BlockSpec((RET_V_WIDTH, D_MODEL), lambda i: (0, 0)),
            pl.BlockSpec((D_MODEL, D_MODEL), lambda i: (0, 0)),
        ],
        out_specs=pl.BlockSpec((tm, D_MODEL), lambda i: (i, 0)),
        out_shape=jax.ShapeDtypeStruct((T, D_MODEL), F32),
        compiler_params=_cparams(("parallel",)),
        name="merge",
    )(a, ro, gts, gts, x, wa, wr, wo)


def _ffn_kernel(xp_ref, x_ref, xn_ref, g_ref, wu_ref, cw_ref, wd_ref, gf_ref,
                o_ref, h_ref, acc_ref, *, tm, u1, n1, n2, final):
    halo = SUBLANES
    loc, n = _local(pl.program_id(0), u1, n1, n2)
    g = g_ref[...]
    h_ref[0:halo, :] = jnp.where(loc == 0, 0.0, _rms(xp_ref[...], g)).astype(BF16)
    h_ref[halo:halo + tm, :] = _rms(x_ref[...], g).astype(BF16)
    h_ref[halo + tm:2 * halo + tm, :] = jnp.where(loc == n - 1, 0.0, _rms(xn_ref[...], g)).astype(BF16)

    def up(c0):
        h = h_ref[...]
        return (jnp.dot(h, wu_ref[:, c0:c0 + FFN_TN], preferred_element_type=F32),
                jnp.dot(h, wu_ref[:, D_FF + c0:D_FF + c0 + FFN_TN], preferred_element_type=F32))

    def conv(u, c0):
        return (u[halo - 1:halo - 1 + tm] * cw_ref[0:1, c0:c0 + FFN_TN]
                + u[halo:halo + tm] * cw_ref[1:2, c0:c0 + FFN_TN]
                + u[halo + 1:halo + 1 + tm] * cw_ref[2:3, c0:c0 + FFN_TN])

    starts = list(range(0, D_FF, FFN_TN))
    nxt = up(starts[0])
    for t, c0 in enumerate(starts):
        ug, uv = nxt
        if t + 1 < len(starts):
            nxt = up(starts[t + 1])
        a = (_gelu_tanh(conv(ug, c0)) * conv(uv, D_FF + c0)).astype(BF16)
        part = jnp.dot(a, wd_ref[c0:c0 + FFN_TN, :], preferred_element_type=F32)
        if t == 0:
            acc_ref[...] = part
        else:
            acc_ref[...] += part

    y = x_ref[...] + acc_ref[...]
    if final:
        y = _rms(y, gf_ref[...])
    o_ref[...] = y


def _ffn(x, g, w_up, conv_w, w_down, gf, tm, u1, n1, n2, final):
    T = x.shape[0]
    hb = tm // SUBLANES
    last = T // SUBLANES - 1
    resident = dict(pipeline_mode=pl.Buffered(1))
    return pl.pallas_call(
        functools.partial(_ffn_kernel, tm=tm, u1=u1, n1=n1, n2=n2, final=final),
        grid=(T // tm,),
        in_specs=[
            pl.BlockSpec((SUBLANES, D_MODEL), lambda i: (jnp.maximum(i * hb - 1, 0), 0)),
            pl.BlockSpec((tm, D_MODEL), lambda i: (i, 0)),
            pl.BlockSpec((SUBLANES, D_MODEL), lambda i: (jnp.minimum((i + 1) * hb, last), 0)),
            pl.BlockSpec((1, D_MODEL), lambda i: (0, 0)),
            pl.BlockSpec((D_MODEL, 2 * D_FF), lambda i: (0, 0), **resident),
            pl.BlockSpec((3, 2 * D_FF), lambda i: (0, 0), **resident),
            pl.BlockSpec((D_FF, D_MODEL), lambda i: (0, 0), **resident),
            pl.BlockSpec((1, D_MODEL), lambda i: (0, 0)),
        ],
        out_specs=pl.BlockSpec((tm, D_MODEL), lambda i: (i, 0)),
        out_shape=jax.ShapeDtypeStruct((T, D_MODEL), F32),
        scratch_shapes=[
            pltpu.VMEM((tm + 2 * SUBLANES, D_MODEL), BF16),
            pltpu.VMEM((tm, D_MODEL), F32),
        ],
        compiler_params=_cparams(("parallel",)),
        name="ffn",
    )(x, x, x, g, w_up, conv_w, w_down, gf)


def _rope_tables(seq_len):
    inv_freq = ROPE_BASE ** (-jnp.arange(0, RET_QK_DIM, 2, dtype=F32) / RET_QK_DIM)
    ang = jnp.arange(seq_len, dtype=F32)[:, None] * inv_freq[None, :]
    cos, sin = jnp.cos(ang), jnp.sin(ang)
    return jnp.concatenate([cos, cos], axis=1), jnp.concatenate([-sin, sin], axis=1)


def _pick_tile(limit, *sizes):
    t = limit
    while any(s % t for s in sizes):
        t //= 2
    return t


def kernel(x_prompt, x_sample, norm_mix_g, w_in, na_rel_bias, ret_decay_fwd, ret_decay_bwd, ret_norm_g,
           w_branch_attn, w_branch_ret, w_out, norm_ffn_g, w_up, ffn_conv_w, w_down, norm_final_g):
    B1, S1, D = x_prompt.shape
    B2, S2, _ = x_sample.shape
    T1, T2 = B1 * S1, B2 * S2
    T = T1 + T2
    depth = w_in.shape[0]
    assert D == D_MODEL and w_in.shape[2] == D_IN
    assert S1 % (NA_WH * GRID_W) == 0 and S2 % (NA_WH * GRID_W) == 0 and T1 % S2 == 0

    x = jnp.concatenate([x_prompt.reshape(T1, D), x_sample.reshape(T2, D)], axis=0)
    cos2, sin2 = _rope_tables(max(S1, S2))

    tm_proj = _pick_tile(512, S1, S2)
    tm_mix = _pick_tile(512, S1, S2)

    def units(t):
        return T1 // t, S1 // t, S2 // t

    for l in range(depth):
        na, rqk, rv, gts = _inproj(x, norm_mix_g[l][None].astype(F32), w_in[l].astype(BF16),
                                   cos2, sin2, tm_proj, *units(tm_proj))
        tab = _na_bias_table(na_rel_bias[l])
        a = _na(na, tab, None, 0, B1, S1, T)
        a = _na(na, tab, a, T1, B2, S2, T)
        ro = _retention(rqk, rv, gts, ret_norm_g[l][None].astype(F32), ret_decay_fwd[l], ret_decay_bwd[l],
                        *units(RET_CHUNK * RET_CHUNKS_PER_STEP))
        x = _merge(a, ro, gts, x, w_branch_attn[l].astype(BF16), w_branch_ret[l].astype(BF16),
                   w_out[l].astype(BF16), tm_mix)
        x = _ffn(x, norm_ffn_g[l][None].astype(F32), w_up[l].astype(BF16), ffn_conv_w[l].astype(F32),
                 w_down[l].astype(BF16), norm_final_g[None].astype(F32), tm_mix, *units(tm_mix),
                 final=(l == depth - 1))

    return x[:T1].reshape(B1, S1, D), x[T1:].reshape(B2, S2, D)
```

```python
import functools

import numpy as np
import jax
import jax.numpy as jnp
from jax import lax
from jax.experimental import pallas as pl
from jax.experimental.pallas import tpu as pltpu

F32 = jnp.float32
BF16 = jnp.bfloat16

D_MODEL = 1024
GRID_W = 64
NA_HEADS = 8
NA_HEAD_DIM = 64
NA_WIDTH = NA_HEADS * NA_HEAD_DIM
NA_WH = 8
NA_WW = 16
RET_HEADS = 4
RET_QK_DIM = 128
RET_V_DIM = 256
RET_QK_WIDTH = RET_HEADS * RET_QK_DIM
RET_V_WIDTH = RET_HEADS * RET_V_DIM
RET_CHUNK = 128
ROPE_BASE = 10000.0
D_FF = 2816
EPS = 1e-6
D_IN = 3 * NA_WIDTH + 2 * RET_QK_WIDTH + 2 * RET_V_WIDTH + 2 * D_MODEL

LANES = 128
SUBLANES = 8
PROJ_TN = 512
FFN_TN = 256
RET_CHUNKS_PER_STEP = 4
NA_ROW_GROUP = 8
MASK_VALUE = -1e30
VMEM_LIMIT = 56 * 1024 * 1024


def _cparams(sem):
    return pltpu.CompilerParams(dimension_semantics=sem, vmem_limit_bytes=VMEM_LIMIT)


def _local(u, u1, n1, n2):
    in1 = u < u1
    loc = jnp.where(in1, lax.rem(u, n1), lax.rem(jnp.maximum(u - u1, 0), n2))
    return loc, jnp.where(in1, n1, n2)


def _rms(x, g):
    ms = jnp.mean(x * x, axis=-1, keepdims=True)
    return (x * lax.rsqrt(ms + EPS)) * g


def _sigmoid(x):
    return 1.0 / (1.0 + jnp.exp(-x))


def _gelu_tanh(x):
    c = np.float32(np.sqrt(2.0 / np.pi))
    return x * (0.5 * (1.0 + jnp.tanh(c * (x + np.float32(0.044715) * (x * x * x)))))


def _x_specs(xs, tm, u1):
    if len(xs) == 1:
        return [pl.BlockSpec((tm, D_MODEL), lambda i: (i, 0))]
    return [pl.BlockSpec((tm, D_MODEL), lambda i: (jnp.minimum(i, u1 - 1), 0)),
            pl.BlockSpec((tm, D_MODEL), lambda i: (jnp.maximum(i - u1, 0), 0))]


def _read_x(x_refs, u1):
    if len(x_refs) == 1:
        return x_refs[0][...]
    return jnp.where(pl.program_id(0) < u1, x_refs[0][...], x_refs[1][...])


def _inproj_kernel(*refs, nx, u1):
    x_refs, (g_ref, w_ref, cos_ref, sin_ref, na_ref, rqk_ref, rv_ref, gts_ref, h_ref) = refs[:nx], refs[nx:]
    h_ref[...] = _rms(_read_x(x_refs, u1), g_ref[...]).astype(BF16)

    def proj(c0):
        return jnp.dot(h_ref[...], w_ref[:, c0:c0 + PROJ_TN], preferred_element_type=F32)

    def rope(acc, c0, scale):
        cos2 = cos_ref[...]
        sin2 = sin_ref[...]
        for hd in range(PROJ_TN // RET_QK_DIM):
            xh = acc[:, hd * RET_QK_DIM:(hd + 1) * RET_QK_DIM]
            r = xh * cos2 + pltpu.roll(xh, RET_QK_DIM // 2, axis=1) * sin2
            if scale is not None:
                r = r * scale
            rqk_ref[:, c0 + hd * RET_QK_DIM:c0 + (hd + 1) * RET_QK_DIM] = r

    col = 0
    for c0 in range(0, 3 * NA_WIDTH, PROJ_TN):
        na_ref[:, c0:c0 + PROJ_TN] = proj(col + c0).astype(BF16)
    col += 3 * NA_WIDTH
    rope(proj(col), 0, None)
    rope(proj(col + RET_QK_WIDTH), RET_QK_WIDTH, np.float32(RET_QK_DIM ** -0.5))
    col += 2 * RET_QK_WIDTH
    for c0 in range(0, RET_V_WIDTH, PROJ_TN):
        rv_ref[:, c0:c0 + PROJ_TN] = proj(col + c0).astype(BF16)
    col += RET_V_WIDTH
    for c0 in range(0, RET_V_WIDTH + 2 * D_MODEL, PROJ_TN):
        gts_ref[:, c0:c0 + PROJ_TN] = proj(col + c0).astype(BF16)


def _inproj(xs, g, w, cos2, sin2, tm, u1, n1, n2):
    T = sum(x.shape[0] for x in xs)
    assert RET_QK_WIDTH == PROJ_TN

    def tab_map(i):
        return (_local(i, u1, n1, n2)[0], 0)

    return pl.pallas_call(
        functools.partial(_inproj_kernel, nx=len(xs), u1=u1),
        grid=(T // tm,),
        in_specs=_x_specs(xs, tm, u1) + [
            pl.BlockSpec((1, D_MODEL), lambda i: (0, 0)),
            pl.BlockSpec((D_MODEL, D_IN), lambda i: (0, 0), pipeline_mode=pl.Buffered(1)),
            pl.BlockSpec((tm, RET_QK_DIM), tab_map),
            pl.BlockSpec((tm, RET_QK_DIM), tab_map),
        ],
        out_specs=[
            pl.BlockSpec((tm, 3 * NA_WIDTH), lambda i: (i, 0)),
            pl.BlockSpec((tm, 2 * RET_QK_WIDTH), lambda i: (i, 0)),
            pl.BlockSpec((tm, RET_V_WIDTH), lambda i: (i, 0)),
            pl.BlockSpec((tm, RET_V_WIDTH + 2 * D_MODEL), lambda i: (i, 0)),
        ],
        out_shape=[
            jax.ShapeDtypeStruct((T, 3 * NA_WIDTH), BF16),
            jax.ShapeDtypeStruct((T, 2 * RET_QK_WIDTH), F32),
            jax.ShapeDtypeStruct((T, RET_V_WIDTH), BF16),
            jax.ShapeDtypeStruct((T, RET_V_WIDTH + 2 * D_MODEL), BF16),
        ],
        scratch_shapes=[pltpu.VMEM((tm, D_MODEL), BF16)],
        compiler_params=_cparams(("parallel",)),
        name="inproj",
    )(*xs, g, w, cos2, sin2)


def _na_kernel(q_ref, k_ref, v_ref, tab_ref, *rest, rows):
    o_ref = rest[-1]
    band = NA_WH * GRID_W
    pair = 2 * GRID_W
    lane = lax.broadcasted_iota(jnp.int32, (pair, LANES), 1)
    sub = lax.broadcasted_iota(jnp.int32, (pair, LANES), 0)
    keep = (lane < NA_HEAD_DIM) == (sub < GRID_W)
    low = lax.broadcasted_iota(jnp.int32, (GRID_W, LANES), 1) < NA_HEAD_DIM
    scale = NA_HEAD_DIM ** -0.5

    def tok(row):
        return pl.ds(pl.multiple_of(row * GRID_W, GRID_W), GRID_W)

    def scores(r):
        rs = jnp.clip(r - NA_WH // 2, 0, rows - NA_WH)
        q = q_ref[tok(r), :] * scale
        kb = k_ref[pl.ds(pl.multiple_of(rs * GRID_W, GRID_W), band), :]
        qm = jnp.where(keep, jnp.concatenate([q, q], axis=0), jnp.zeros((pair, LANES), BF16))
        s = lax.dot_general(qm, kb, (((1,), (1,)), ((), ())), preferred_element_type=F32)
        return s + tab_ref[0, r - rs]

    def softmax(s):
        p = jnp.exp(s - jnp.max(s, axis=-1, keepdims=True))
        return p.astype(BF16), jnp.sum(p, axis=-1, keepdims=True)

    def weighted(r, p, l):
        rs = jnp.clip(r - NA_WH // 2, 0, rows - NA_WH)
        vb = v_ref[pl.ds(pl.multiple_of(rs * GRID_W, GRID_W), band), :]
        o = jnp.dot(p, vb, preferred_element_type=F32) / l
        o_ref[tok(r), :] = jnp.where(low, o[:GRID_W], o[GRID_W:]).astype(BF16)

    def body(g, carry):
        rs_ = [g * NA_ROW_GROUP + i for i in range(NA_ROW_GROUP)]
        ss = [scores(r) for r in rs_]
        pls = [softmax(s) for s in ss]
        for r, (p, l) in zip(rs_, pls):
            weighted(r, p, l)
        return carry

    lax.fori_loop(0, rows // NA_ROW_GROUP, body, 0)


def _na(na, tab, prev, tok0, nseq, S, T):
    rows = S // GRID_W
    b0 = tok0 // S
    npair = NA_HEADS // 2
    in_specs = [
        pl.BlockSpec((S, LANES), lambda b, hp: (b0 + b, hp)),
        pl.BlockSpec((S, LANES), lambda b, hp: (b0 + b, npair + hp)),
        pl.BlockSpec((S, LANES), lambda b, hp: (b0 + b, 2 * npair + hp)),
        pl.BlockSpec((1, NA_WH, 2 * GRID_W, NA_WH * GRID_W), lambda b, hp: (hp, 0, 0, 0)),
    ]
    args = [na, na, na, tab]
    aliases = {}
    if prev is not None:
        in_specs.append(pl.BlockSpec(memory_space=pl.ANY))
        args.append(prev)
        aliases = {4: 0}
    return pl.pallas_call(
        functools.partial(_na_kernel, rows=rows),
        grid=(nseq, npair),
        in_specs=in_specs,
        out_specs=pl.BlockSpec((S, LANES), lambda b, hp: (b0 + b, hp)),
        out_shape=jax.ShapeDtypeStruct((T, NA_WIDTH), BF16),
        input_output_aliases=aliases,
        compiler_params=_cparams(("parallel", "parallel")),
        name="na",
    )(*args)


def _na_bias_table(rel_bias):
    H = rel_bias.shape[0]
    col = np.arange(GRID_W)
    col_start = np.clip(col - NA_WW // 2, 0, GRID_W - NA_WW)
    valid = (col[None, :] >= col_start[:, None]) & (col[None, :] < col_start[:, None] + NA_WW)
    rel = col[None, :] - col[:, None] + (NA_WW - 1)
    pick = (valid[:, :, None] & (rel[:, :, None] == np.arange(2 * NA_WW - 1))).astype(np.float32)
    cm = jnp.einsum('hij,ckj->hick', rel_bias.astype(F32), pick, precision=lax.Precision.HIGHEST)
    cm = jnp.where(valid[None, None], cm, MASK_VALUE)
    per_d = [jnp.transpose(cm[:, NA_WH - 1 - d:2 * NA_WH - 1 - d], (0, 2, 1, 3)) for d in range(NA_WH)]
    t = jnp.stack(per_d, axis=1).reshape(H // 2, 2, NA_WH, GRID_W, NA_WH * GRID_W)
    return jnp.transpose(t, (0, 2, 1, 3, 4)).reshape(H // 2, NA_WH, 2 * GRID_W, NA_WH * GRID_W)


def _ret_bwd_kernel(cd_ref, k_ref, v_ref, kdb_ref, sb_ref, st_ref, *, u1, n1, n2):
    step = pl.num_programs(0) - 1 - pl.program_id(0)
    loc, n = _local(step, u1, n1, n2)

    @pl.when(loc == n - 1)
    def _():
        st_ref[...] = jnp.zeros_like(st_ref)

    for cc in reversed(range(RET_CHUNKS_PER_STEP)):
        tok = slice(cc * RET_CHUNK, (cc + 1) * RET_CHUNK)
        for h in range(RET_HEADS):
            st = st_ref[h]
            sb_ref[cc, h] = st.astype(BF16)
            kd = (k_ref[tok, h * RET_QK_DIM:(h + 1) * RET_QK_DIM] * kdb_ref[h]).astype(BF16)
            vh = v_ref[tok, h * RET_V_DIM:(h + 1) * RET_V_DIM]
            st_ref[h] = st * cd_ref[1, h] + lax.dot_general(
                kd, vh, (((0,), (0,)), ((), ())), preferred_element_type=F32)


def _ret_fwd_kernel(cd_ref, q_ref, k_ref, v_ref, sb_ref, rg_ref, gn_ref, dm_ref, qdf_ref, qdb_ref,
                    kdf_ref, o_ref, st_ref, *, u1, n1, n2):
    loc, _ = _local(pl.program_id(0), u1, n1, n2)

    @pl.when(loc == 0)
    def _():
        st_ref[...] = jnp.zeros_like(st_ref)

    for cc in range(RET_CHUNKS_PER_STEP):
        tok = slice(cc * RET_CHUNK, (cc + 1) * RET_CHUNK)
        for h in range(RET_HEADS):
            qh = q_ref[tok, h * RET_QK_DIM:(h + 1) * RET_QK_DIM]
            kh = k_ref[tok, h * RET_QK_DIM:(h + 1) * RET_QK_DIM]
            vh = v_ref[tok, h * RET_V_DIM:(h + 1) * RET_V_DIM]
            st = st_ref[h]
            s = lax.dot_general(qh.astype(BF16), kh.astype(BF16), (((1,), (1,)), ((), ())),
                                preferred_element_type=F32)
            o = jnp.dot((s * dm_ref[h]).astype(BF16), vh, preferred_element_type=F32)
            o = o + jnp.dot((qh * qdf_ref[h]).astype(BF16), st.astype(BF16), preferred_element_type=F32)
            o = o + jnp.dot((qh * qdb_ref[h]).astype(BF16), sb_ref[cc, h], preferred_element_type=F32)
            st_ref[h] = st * cd_ref[0, h] + lax.dot_general(
                (kh * kdf_ref[h]).astype(BF16), vh, (((0,), (0,)), ((), ())), preferred_element_type=F32)
            mu = jnp.mean(o, axis=-1, keepdims=True)
            oc = o - mu
            var = jnp.mean(oc * oc, axis=-1, keepdims=True)
            y = (oc * lax.rsqrt(var + EPS)) * gn_ref[:, h * RET_V_DIM:(h + 1) * RET_V_DIM]
            rg = rg_ref[tok, h * RET_V_DIM:(h + 1) * RET_V_DIM].astype(F32)
            o_ref[tok, h * RET_V_DIM:(h + 1) * RET_V_DIM] = ((rg * _sigmoid(rg)) * y).astype(BF16)


def _retention(rqk, rv, gts, gn, decay_f, decay_b, u1, n1, n2):
    T = rqk.shape[0]
    C = RET_CHUNK
    cps = RET_CHUNKS_PER_STEP
    nc = T // (C * cps)
    lgf = jax.nn.log_sigmoid(decay_f.astype(F32))
    lgb = jax.nn.log_sigmoid(decay_b.astype(F32))
    idx = jnp.arange(C, dtype=F32)
    diff = idx[:, None] - idx[None, :]
    dmat = jnp.where(diff[None] >= 0,
                     jnp.exp(jnp.maximum(diff, 0.0)[None] * lgf[:, None, None]),
                     jnp.exp(jnp.maximum(-diff, 0.0)[None] * lgb[:, None, None]))

    def rows(e):
        return jnp.broadcast_to(e[:, :, None], (RET_HEADS, C, RET_QK_DIM))

    qdf = rows(jnp.exp((idx + 1.0)[None, :] * lgf[:, None]))
    kdf = rows(jnp.exp((C - 1.0 - idx)[None, :] * lgf[:, None]))
    qdb = rows(jnp.exp((C - idx)[None, :] * lgb[:, None]))
    kdb = rows(jnp.exp(idx[None, :] * lgb[:, None]))
    cd = jnp.stack([jnp.exp(C * lgf), jnp.exp(C * lgb)])

    smem = pl.BlockSpec(memory_space=pltpu.SMEM)
    full3 = pl.BlockSpec((RET_HEADS, C, RET_QK_DIM), lambda c: (0, 0, 0))
    state = pltpu.VMEM((RET_HEADS, RET_QK_DIM, RET_V_DIM), F32)

    sb = pl.pallas_call(
        functools.partial(_ret_bwd_kernel, u1=u1, n1=n1, n2=n2),
        grid=(nc,),
        in_specs=[
            smem,
            pl.BlockSpec((cps * C, RET_QK_WIDTH), lambda c: (nc - 1 - c, 1)),
            pl.BlockSpec((cps * C, RET_V_WIDTH), lambda c: (nc - 1 - c, 0)),
            full3,
        ],
        out_specs=pl.BlockSpec((cps, RET_HEADS, RET_QK_DIM, RET_V_DIM), lambda c: (nc - 1 - c, 0, 0, 0)),
        out_shape=jax.ShapeDtypeStruct((nc * cps, RET_HEADS, RET_QK_DIM, RET_V_DIM), BF16),
        scratch_shapes=[state],
        compiler_params=_cparams(("arbitrary",)),
        name="ret_bwd",
    )(cd, rqk, rv, kdb)

    return pl.pallas_call(
        functools.partial(_ret_fwd_kernel, u1=u1, n1=n1, n2=n2),
        grid=(nc,),
        in_specs=[
            smem,
            pl.BlockSpec((cps * C, RET_QK_WIDTH), lambda c: (c, 0)),
            pl.BlockSpec((cps * C, RET_QK_WIDTH), lambda c: (c, 1)),
            pl.BlockSpec((cps * C, RET_V_WIDTH), lambda c: (c, 0)),
            pl.BlockSpec((cps, RET_HEADS, RET_QK_DIM, RET_V_DIM), lambda c: (c, 0, 0, 0)),
            pl.BlockSpec((cps * C, RET_V_WIDTH), lambda c: (c, 0)),
            pl.BlockSpec((1, RET_V_WIDTH), lambda c: (0, 0)),
            pl.BlockSpec((RET_HEADS, C, C), lambda c: (0, 0, 0)),
            full3, full3, full3,
        ],
        out_specs=pl.BlockSpec((cps * C, RET_V_WIDTH), lambda c: (c, 0)),
        out_shape=jax.ShapeDtypeStruct((T, RET_V_WIDTH), BF16),
        scratch_shapes=[state],
        compiler_params=_cparams(("arbitrary",)),
        name="ret_fwd",
    )(cd, rqk, rqk, rv, sb, gts, gn, dmat, qdf, qdb, kdf)


def _merge_kernel(*refs, nx, u1):
    x_refs, (a_ref, ro_ref, ga_ref, gr_ref, wa_ref, wr_ref, wo_ref, o_ref) = refs[:nx], refs[nx:]
    ya = jnp.dot(a_ref[...], wa_ref[...], preferred_element_type=F32)
    yr = jnp.dot(ro_ref[...], wr_ref[...], preferred_element_type=F32)
    mixed = _sigmoid(ga_ref[...].astype(F32)) * ya + _sigmoid(gr_ref[...].astype(F32)) * yr
    o_ref[...] = _read_x(x_refs, u1) + jnp.dot(mixed.astype(BF16), wo_ref[...], preferred_element_type=F32)


def _merge(a, ro, gts, xs, wa, wr, wo, tm, u1):
    T = a.shape[0]
    return pl.pallas_call(
        functools.partial(_merge_kernel, nx=len(xs), u1=u1),
        grid=(T // tm,),
        in_specs=_x_specs(xs, tm, u1) + [
            pl.BlockSpec((tm, NA_WIDTH), lambda i: (i, 0)),
            pl.BlockSpec((tm, RET_V_WIDTH), lambda i: (i, 0)),
            pl.BlockSpec((tm, D_MODEL), lambda i: (i, 1)),
            pl.BlockSpec((tm, D_MODEL), lambda i: (i, 2)),
            pl.BlockSpec((NA_WIDTH, D_MODEL), lambda i: (0, 0)),
            pl.BlockSpec((RET_V_WIDTH, D_MODEL), lambda i: (0, 0)),
            pl.BlockSpec((D_MODEL, D_MODEL), lambda i: (0, 0)),
        ],
        out_specs=pl.BlockSpec((tm, D_MODEL), lambda i: (i, 0)),
        out_shape=jax.ShapeDtypeStruct((T, D_MODEL), F32),
        compiler_params=_cparams(("parallel",)),
        name="merge",
    )(*xs, a, ro, gts, gts, wa, wr, wo)


def _ffn_kernel(xp_ref, x_ref, xn_ref, g_ref, wu_ref, cw_ref, wd_ref, gf_ref, *rest, tm, u1, n1, n2, final):
    o_refs, (h_ref, a_ref, xs_ref) = rest[:-3], rest[-3:]
    sub = SUBLANES
    npl = tm // sub
    loc, n = _local(pl.program_id(0), u1, n1, n2)
    g = g_ref[...]

    xs_ref[...] = jnp.swapaxes(x_ref[...].reshape(sub, npl, D_MODEL), 0, 1).reshape(tm, D_MODEL)
    h_ref[0:tm, :] = _rms(xs_ref[...], g).astype(BF16)
    hp = jnp.where(loc == 0, 0.0, _rms(xp_ref[...], g))[sub - 1:sub]
    hn = jnp.where(loc == n - 1, 0.0, _rms(xn_ref[...], g))[0:1]
    hrow = lax.broadcasted_iota(jnp.int32, (2 * sub, D_MODEL), 0)
    h_ref[tm:tm + 2 * sub, :] = jnp.where(hrow == 0, hp, jnp.where(hrow == 1, hn, 0.0)).astype(BF16)

    def up(c0):
        h = h_ref[...]
        return (jnp.dot(h, wu_ref[:, c0:c0 + FFN_TN], preferred_element_type=F32),
                jnp.dot(h, wu_ref[:, D_FF + c0:D_FF + c0 + FFN_TN], preferred_element_type=F32))

    row = lax.broadcasted_iota(jnp.int32, (sub, FFN_TN), 0)

    def conv(u, c0):
        first = jnp.where(row == 0, u[tm:tm + 1], pltpu.roll(u[tm - sub:tm], 1, axis=0))
        last = jnp.where(row == sub - 1, u[tm + 1:tm + 2], pltpu.roll(u[0:sub], sub - 1, axis=0))
        before = jnp.concatenate([first, u[0:tm - sub]], axis=0)
        after = jnp.concatenate([u[sub:tm], last], axis=0)
        return (before * cw_ref[0:1, c0:c0 + FFN_TN] + u[0:tm] * cw_ref[1:2, c0:c0 + FFN_TN]
                + after * cw_ref[2:3, c0:c0 + FFN_TN])

    starts = list(range(0, D_FF, FFN_TN))
    nxt = up(starts[0])
    for t, c0 in enumerate(starts):
        ug, uv = nxt
        if t + 1 < len(starts):
            nxt = up(starts[t + 1])
        a_ref[:, c0:c0 + FFN_TN] = (_gelu_tanh(conv(ug, c0)) * conv(uv, D_FF + c0)).astype(BF16)

    y = jnp.dot(a_ref[...], wd_ref[...], preferred_element_type=F32)
    y = xs_ref[...] + y
    if final:
        y = _rms(y, gf_ref[...])
    y = jnp.swapaxes(y.reshape(npl, sub, D_MODEL), 0, 1).reshape(tm, D_MODEL)
    if len(o_refs) == 1:
        o_refs[0][...] = y
    else:
        @pl.when(pl.program_id(0) < u1)
        def _():
            o_refs[0][...] = y

        @pl.when(pl.program_id(0) >= u1)
        def _():
            o_refs[1][...] = y


def _ffn(x, g, w_up, conv_w, w_down, gf, tm, u1, n1, n2, final):
    T = x.shape[0]
    hb = tm // SUBLANES
    last = T // SUBLANES - 1
    resident = dict(pipeline_mode=pl.Buffered(1))
    if final:
        out_specs = [pl.BlockSpec((tm, D_MODEL), lambda i: (jnp.minimum(i, u1 - 1), 0)),
                     pl.BlockSpec((tm, D_MODEL), lambda i: (jnp.maximum(i - u1, 0), 0))]
        out_shape = [jax.ShapeDtypeStruct((u1 * tm, D_MODEL), F32),
                     jax.ShapeDtypeStruct((T - u1 * tm, D_MODEL), F32)]
    else:
        out_specs = pl.BlockSpec((tm, D_MODEL), lambda i: (i, 0))
        out_shape = jax.ShapeDtypeStruct((T, D_MODEL), F32)
    return pl.pallas_call(
        functools.partial(_ffn_kernel, tm=tm, u1=u1, n1=n1, n2=n2, final=final),
        grid=(T // tm,),
        in_specs=[
            pl.BlockSpec((SUBLANES, D_MODEL), lambda i: (jnp.maximum(i * hb - 1, 0), 0)),
            pl.BlockSpec((tm, D_MODEL), lambda i: (i, 0)),
            pl.BlockSpec((SUBLANES, D_MODEL), lambda i: (jnp.minimum((i + 1) * hb, last), 0)),
            pl.BlockSpec((1, D_MODEL), lambda i: (0, 0)),
            pl.BlockSpec((D_MODEL, 2 * D_FF), lambda i: (0, 0), **resident),
            pl.BlockSpec((3, 2 * D_FF), lambda i: (0, 0), **resident),
            pl.BlockSpec((D_FF, D_MODEL), lambda i: (0, 0), **resident),
            pl.BlockSpec((1, D_MODEL), lambda i: (0, 0)),
        ],
        out_specs=out_specs,
        out_shape=out_shape,
        scratch_shapes=[
            pltpu.VMEM((tm + 2 * SUBLANES, D_MODEL), BF16),
            pltpu.VMEM((tm, D_FF), BF16),
            pltpu.VMEM((tm, D_MODEL), F32),
        ],
        compiler_params=_cparams(("arbitrary",)),
        name="ffn",
    )(x, x, x, g, w_up, conv_w, w_down, gf)


def _rope_tables(seq_len):
    inv_freq = ROPE_BASE ** (-jnp.arange(0, RET_QK_DIM, 2, dtype=F32) / RET_QK_DIM)
    ang = jnp.arange(seq_len, dtype=F32)[:, None] * inv_freq[None, :]
    cos, sin = jnp.cos(ang), jnp.sin(ang)
    return jnp.concatenate([cos, cos], axis=1), jnp.concatenate([-sin, sin], axis=1)


def _pick_tile(limit, *sizes):
    t = limit
    while any(s % t for s in sizes):
        t //= 2
    return t


def kernel(x_prompt, x_sample, norm_mix_g, w_in, na_rel_bias, ret_decay_fwd, ret_decay_bwd, ret_norm_g,
           w_branch_attn, w_branch_ret, w_out, norm_ffn_g, w_up, ffn_conv_w, w_down, norm_final_g):
    B1, S1, D = x_prompt.shape
    B2, S2, _ = x_sample.shape
    T1, T2 = B1 * S1, B2 * S2
    T = T1 + T2
    depth = w_in.shape[0]
    assert D == D_MODEL and w_in.shape[2] == D_IN
    assert S1 % (NA_WH * GRID_W) == 0 and S2 % (NA_WH * GRID_W) == 0 and T1 % S2 == 0

    xs = (x_prompt.reshape(T1, D), x_sample.reshape(T2, D))
    cos2, sin2 = _rope_tables(max(S1, S2))

    tm_proj = _pick_tile(512, S1, S2)
    tm_mix = _pick_tile(512, S1, S2)

    def units(t):
        return T1 // t, S1 // t, S2 // t

    for l in range(depth):
        na, rqk, rv, gts = _inproj(xs, norm_mix_g[l][None].astype(F32), w_in[l].astype(BF16),
                                   cos2, sin2, tm_proj, *units(tm_proj))
        tab = _na_bias_table(na_rel_bias[l])
        a = _na(na, tab, None, 0, B1, S1, T)
        a = _na(na, tab, a, T1, B2, S2, T)
        ro = _retention(rqk, rv, gts, ret_norm_g[l][None].astype(F32), ret_decay_fwd[l], ret_decay_bwd[l],
                        *units(RET_CHUNK * RET_CHUNKS_PER_STEP))
        x = _merge(a, ro, gts, xs, w_branch_attn[l].astype(BF16), w_branch_ret[l].astype(BF16),
                   w_out[l].astype(BF16), tm_mix, T1 // tm_mix)
        x = _ffn(x, norm_ffn_g[l][None].astype(F32), w_up[l].astype(BF16), ffn_conv_w[l].astype(F32),
                 w_down[l].astype(BF16), norm_final_g[None].astype(F32), tm_mix, *units(tm_mix),
                 final=(l == depth - 1))
        xs = (x,)

    y1, y2 = x
    return y1.reshape(B1, S1, D), y2.reshape(B2, S2, D)
```

```python
import functools

import numpy as np
import jax
import jax.numpy as jnp
from jax import lax
from jax.experimental import pallas as pl
from jax.experimental.pallas import tpu as pltpu

F32 = jnp.float32
BF16 = jnp.bfloat16

D_MODEL = 1024
GRID_W = 64
NA_HEADS = 8
NA_HEAD_DIM = 64
NA_WIDTH = NA_HEADS * NA_HEAD_DIM
NA_WH = 8
NA_WW = 16
RET_HEADS = 4
RET_QK_DIM = 128
RET_V_DIM = 256
RET_QK_WIDTH = RET_HEADS * RET_QK_DIM
RET_V_WIDTH = RET_HEADS * RET_V_DIM
RET_CHUNK = 128
ROPE_BASE = 10000.0
D_FF = 2816
EPS = 1e-6
D_IN = 3 * NA_WIDTH + 2 * RET_QK_WIDTH + 2 * RET_V_WIDTH + 2 * D_MODEL

LANES = 128
SUBLANES = 8
PROJ_TN = 512
FFN_TN = 256
RET_CHUNKS_PER_STEP = 4
NA_ROW_GROUP = 16
MASK_VALUE = -1e30
VMEM_LIMIT = 56 * 1024 * 1024


def _cparams(sem):
    return pltpu.CompilerParams(dimension_semantics=sem, vmem_limit_bytes=VMEM_LIMIT)


def _local(u, u1, n1, n2):
    in1 = u < u1
    loc = jnp.where(in1, lax.rem(u, n1), lax.rem(jnp.maximum(u - u1, 0), n2))
    return loc, jnp.where(in1, n1, n2)


def _rms(x, g):
    ms = jnp.mean(x * x, axis=-1, keepdims=True)
    return (x * lax.rsqrt(ms + EPS)) * g


def _sigmoid(x):
    return 0.5 * (jnp.tanh(0.5 * x) + 1.0)


def _gelu_tanh(x):
    c = np.float32(np.sqrt(2.0 / np.pi))
    return x * (0.5 * (1.0 + jnp.tanh(c * (x + np.float32(0.044715) * (x * x * x)))))


def _x_specs(xs, tm, u1):
    width = xs[0].shape[1]
    if len(xs) == 1:
        return [pl.BlockSpec((tm, width), lambda i: (i, 0))]
    return [pl.BlockSpec((tm, width), lambda i: (jnp.minimum(i, u1 - 1), 0)),
            pl.BlockSpec((tm, width), lambda i: (jnp.maximum(i - u1, 0), 0))]


def _read_x(x_refs, u1):
    if len(x_refs) == 1:
        return x_refs[0][...]
    return jnp.where(pl.program_id(0) < u1, x_refs[0][...], x_refs[1][...])


def _inproj_kernel(*refs, nx, u1):
    x_refs, (g_ref, w_ref, cos_ref, sin_ref, na_ref, rqk_ref, rv_ref, gts_ref, h_ref) = refs[:nx], refs[nx:]
    h_ref[...] = _rms(_read_x(x_refs, u1), g_ref[...]).astype(BF16)

    def proj(c0):
        return jnp.dot(h_ref[...], w_ref[:, c0:c0 + PROJ_TN], preferred_element_type=F32)

    def rope(acc, c0, scale):
        cos2 = cos_ref[...]
        sin2 = sin_ref[...]
        for hd in range(PROJ_TN // RET_QK_DIM):
            xh = acc[:, hd * RET_QK_DIM:(hd + 1) * RET_QK_DIM]
            r = xh * cos2 + pltpu.roll(xh, RET_QK_DIM // 2, axis=1) * sin2
            if scale is not None:
                r = r * scale
            rqk_ref[:, c0 + hd * RET_QK_DIM:c0 + (hd + 1) * RET_QK_DIM] = r

    col = 0
    for c0 in range(0, 3 * NA_WIDTH, PROJ_TN):
        na_ref[:, c0:c0 + PROJ_TN] = proj(col + c0).astype(BF16)
    col += 3 * NA_WIDTH
    rope(proj(col), 0, None)
    rope(proj(col + RET_QK_WIDTH), RET_QK_WIDTH, np.float32(RET_QK_DIM ** -0.5))
    col += 2 * RET_QK_WIDTH
    for c0 in range(0, RET_V_WIDTH, PROJ_TN):
        rv_ref[:, c0:c0 + PROJ_TN] = proj(col + c0).astype(BF16)
    col += RET_V_WIDTH
    for c0 in range(0, RET_V_WIDTH + 2 * D_MODEL, PROJ_TN):
        gts_ref[:, c0:c0 + PROJ_TN] = proj(col + c0).astype(BF16)


def _inproj(xs, g, w, cos2, sin2, tm, u1, n1, n2):
    T = sum(x.shape[0] for x in xs)
    assert RET_QK_WIDTH == PROJ_TN

    def tab_map(i):
        return (_local(i, u1, n1, n2)[0], 0)

    return pl.pallas_call(
        functools.partial(_inproj_kernel, nx=len(xs), u1=u1),
        grid=(T // tm,),
        in_specs=_x_specs(xs, tm, u1) + [
            pl.BlockSpec((1, D_MODEL), lambda i: (0, 0)),
            pl.BlockSpec((D_MODEL, D_IN), lambda i: (0, 0), pipeline_mode=pl.Buffered(1)),
            pl.BlockSpec((tm, RET_QK_DIM), tab_map),
            pl.BlockSpec((tm, RET_QK_DIM), tab_map),
        ],
        out_specs=[
            pl.BlockSpec((tm, 3 * NA_WIDTH), lambda i: (i, 0)),
            pl.BlockSpec((tm, 2 * RET_QK_WIDTH), lambda i: (i, 0)),
            pl.BlockSpec((tm, RET_V_WIDTH), lambda i: (i, 0)),
            pl.BlockSpec((tm, RET_V_WIDTH + 2 * D_MODEL), lambda i: (i, 0)),
        ],
        out_shape=[
            jax.ShapeDtypeStruct((T, 3 * NA_WIDTH), BF16),
            jax.ShapeDtypeStruct((T, 2 * RET_QK_WIDTH), F32),
            jax.ShapeDtypeStruct((T, RET_V_WIDTH), BF16),
            jax.ShapeDtypeStruct((T, RET_V_WIDTH + 2 * D_MODEL), BF16),
        ],
        scratch_shapes=[pltpu.VMEM((tm, D_MODEL), BF16)],
        compiler_params=_cparams(("parallel",)),
        name="inproj",
    )(*xs, g, w, cos2, sin2)


def _na_kernel(q_ref, k_ref, v_ref, tab_ref, o_ref, *, rows):
    pair = 2 * GRID_W
    lane = lax.broadcasted_iota(jnp.int32, (pair, LANES), 1)
    sub = lax.broadcasted_iota(jnp.int32, (pair, LANES), 0)
    keep = (lane < NA_HEAD_DIM) == (sub < GRID_W)
    low = lax.broadcasted_iota(jnp.int32, (SUBLANES, LANES), 1) < NA_HEAD_DIM
    scale = NA_HEAD_DIM ** -0.5

    def tok(row):
        return pl.ds(pl.multiple_of(row * GRID_W, GRID_W), GRID_W)

    ncb = GRID_W // NA_WW
    qcol = np.arange(GRID_W)
    qstart = np.clip(qcol - NA_WW // 2, 0, GRID_W - NA_WW)
    live = []
    for i in range(GRID_W // SUBLANES):
        lo, hi = qstart[i * SUBLANES], qstart[(i + 1) * SUBLANES - 1] + NA_WW
        live.append([j for j in range(ncb) if j * NA_WW < hi and (j + 1) * NA_WW > lo])

    def band_rows(ref, rs):
        rows_ = ref[pl.ds(pl.multiple_of(rs * GRID_W, GRID_W), NA_WH * GRID_W), :]
        return jnp.concatenate(
            [rows_[w * GRID_W + j * NA_WW:w * GRID_W + (j + 1) * NA_WW]
             for j in range(ncb) for w in range(NA_WH)], axis=0)

    def scores(r):
        rs = jnp.clip(r - NA_WH // 2, 0, rows - NA_WH)
        q = q_ref[tok(r), :] * scale
        qm = jnp.where(keep, jnp.concatenate([q, q], axis=0), jnp.zeros((pair, LANES), BF16))
        s = lax.dot_general(qm, band_rows(k_ref, rs), (((1,), (1,)), ((), ())), preferred_element_type=F32)
        d = r - rs
        tiles = {}
        for b in range(pair // SUBLANES):
            for j in live[b % (GRID_W // SUBLANES)]:
                rows8, lanes = slice(b * SUBLANES, (b + 1) * SUBLANES), slice(j * LANES, (j + 1) * LANES)
                tiles[b, j] = s[rows8, lanes] + tab_ref[0, d, rows8, lanes]
        return tiles

    def softmax(tiles):
        p, ls = {}, []
        for b in range(pair // SUBLANES):
            mine = [tiles[b, j] for j in live[b % (GRID_W // SUBLANES)]]
            m = jnp.max(functools.reduce(jnp.maximum, mine), axis=-1, keepdims=True)
            es = [jnp.exp(t - m) for t in mine]
            for j, e in zip(live[b % (GRID_W // SUBLANES)], es):
                p[b, j] = e
            ls.append(jnp.sum(functools.reduce(jnp.add, es), axis=-1, keepdims=True))
        zero = jnp.zeros((SUBLANES, LANES), F32)
        full = jnp.concatenate(
            [jnp.concatenate([p.get((b, j), zero) for j in range(ncb)], axis=1)
             for b in range(pair // SUBLANES)], axis=0)
        return full.astype(BF16), ls

    def weighted(r, p, ls):
        rs = jnp.clip(r - NA_WH // 2, 0, rows - NA_WH)
        o = jnp.dot(p, band_rows(v_ref, rs), preferred_element_type=F32)
        nb = GRID_W // SUBLANES
        out = [jnp.where(low,
                         o[i * SUBLANES:(i + 1) * SUBLANES] / ls[i],
                         o[(nb + i) * SUBLANES:(nb + i + 1) * SUBLANES] / ls[nb + i]) for i in range(nb)]
        o_ref[tok(r), :] = jnp.concatenate(out, axis=0).astype(BF16)

    def body(g, carry):
        rs_ = [g * NA_ROW_GROUP + i for i in range(NA_ROW_GROUP)]
        ss = [scores(r) for r in rs_]
        pls = [softmax(s) for s in ss]
        for r, (p, l) in zip(rs_, pls):
            weighted(r, p, l)
        return carry

    lax.fori_loop(0, rows // NA_ROW_GROUP, body, 0)


def _na(na, tab, tok0, nseq, S):
    rows = S // GRID_W
    b0 = tok0 // S
    npair = NA_HEADS // 2
    return pl.pallas_call(
        functools.partial(_na_kernel, rows=rows),
        grid=(nseq, npair),
        in_specs=[
            pl.BlockSpec((S, LANES), lambda b, hp: (b0 + b, hp)),
            pl.BlockSpec((S, LANES), lambda b, hp: (b0 + b, npair + hp)),
            pl.BlockSpec((S, LANES), lambda b, hp: (b0 + b, 2 * npair + hp)),
            pl.BlockSpec((1, NA_WH, 2 * GRID_W, NA_WH * GRID_W), lambda b, hp: (hp, 0, 0, 0)),
        ],
        out_specs=pl.BlockSpec((S, LANES), lambda b, hp: (b, hp)),
        out_shape=jax.ShapeDtypeStruct((nseq * S, NA_WIDTH), BF16),
        compiler_params=_cparams(("parallel", "parallel")),
        name="na",
    )(na, na, na, tab)


def _na_bias_table(rel_bias):
    H = rel_bias.shape[0]
    col = np.arange(GRID_W)
    col_start = np.clip(col - NA_WW // 2, 0, GRID_W - NA_WW)
    valid = (col[None, :] >= col_start[:, None]) & (col[None, :] < col_start[:, None] + NA_WW)
    rel = col[None, :] - col[:, None] + (NA_WW - 1)
    pick = (valid[:, :, None] & (rel[:, :, None] == np.arange(2 * NA_WW - 1))).astype(np.float32)
    cm = jnp.einsum('hij,ckj->hick', rel_bias.astype(F32), pick, precision=lax.Precision.HIGHEST)
    cm = jnp.where(valid[None, None], cm, MASK_VALUE)
    per_d = [jnp.transpose(cm[:, NA_WH - 1 - d:2 * NA_WH - 1 - d], (0, 2, 1, 3)) for d in range(NA_WH)]
    t = jnp.stack(per_d, axis=1)
    t = t.reshape(H, NA_WH, GRID_W, NA_WH, GRID_W // NA_WW, NA_WW).transpose(0, 1, 2, 4, 3, 5)
    t = t.reshape(H // 2, 2, NA_WH, GRID_W, NA_WH * GRID_W)
    return jnp.transpose(t, (0, 2, 1, 3, 4)).reshape(H // 2, NA_WH, 2 * GRID_W, NA_WH * GRID_W)


def _ret_bwd_kernel(cd_ref, k_ref, v_ref, kdb_ref, sb_ref, st_ref, *, u1, n1, n2):
    step = pl.num_programs(0) - 1 - pl.program_id(0)
    loc, n = _local(step, u1, n1, n2)

    @pl.when(loc == n - 1)
    def _():
        st_ref[...] = jnp.zeros_like(st_ref)

    for cc in reversed(range(RET_CHUNKS_PER_STEP)):
        tok = slice(cc * RET_CHUNK, (cc + 1) * RET_CHUNK)
        for h in range(RET_HEADS):
            st = st_ref[h]
            sb_ref[cc, h] = st.astype(BF16)
            kd = (k_ref[tok, h * RET_QK_DIM:(h + 1) * RET_QK_DIM] * kdb_ref[h]).astype(BF16)
            vh = v_ref[tok, h * RET_V_DIM:(h + 1) * RET_V_DIM]
            st_ref[h] = st * cd_ref[1, h] + lax.dot_general(
                kd, vh, (((0,), (0,)), ((), ())), preferred_element_type=F32)


def _ret_fwd_kernel(cd_ref, q_ref, k_ref, v_ref, sb_ref, rg_ref, gn_ref, dm_ref, qdf_ref, qdb_ref,
                    kdf_ref, o_ref, st_ref, *, u1, n1, n2):
    loc, _ = _local(pl.program_id(0), u1, n1, n2)

    @pl.when(loc == 0)
    def _():
        st_ref[...] = jnp.zeros_like(st_ref)

    for cc in range(RET_CHUNKS_PER_STEP):
        tok = slice(cc * RET_CHUNK, (cc + 1) * RET_CHUNK)
        for h in range(RET_HEADS):
            qh = q_ref[tok, h * RET_QK_DIM:(h + 1) * RET_QK_DIM]
            kh = k_ref[tok, h * RET_QK_DIM:(h + 1) * RET_QK_DIM]
            vh = v_ref[tok, h * RET_V_DIM:(h + 1) * RET_V_DIM]
            st = st_ref[h]
            s = lax.dot_general(qh.astype(BF16), kh.astype(BF16), (((1,), (1,)), ((), ())),
                                preferred_element_type=F32)
            o = jnp.dot((s * dm_ref[h]).astype(BF16), vh, preferred_element_type=F32)
            o = o + jnp.dot((qh * qdf_ref[h]).astype(BF16), st.astype(BF16), preferred_element_type=F32)
            o = o + jnp.dot((qh * qdb_ref[h]).astype(BF16), sb_ref[cc, h], preferred_element_type=F32)
            st_ref[h] = st * cd_ref[0, h] + lax.dot_general(
                (kh * kdf_ref[h]).astype(BF16), vh, (((0,), (0,)), ((), ())), preferred_element_type=F32)
            mu = jnp.mean(o, axis=-1, keepdims=True)
            oc = o - mu
            var = jnp.mean(oc * oc, axis=-1, keepdims=True)
            y = (oc * lax.rsqrt(var + EPS)) * gn_ref[:, h * RET_V_DIM:(h + 1) * RET_V_DIM]
            rg = rg_ref[tok, h * RET_V_DIM:(h + 1) * RET_V_DIM].astype(F32)
            o_ref[tok, h * RET_V_DIM:(h + 1) * RET_V_DIM] = ((rg * _sigmoid(rg)) * y).astype(BF16)


def _retention(rqk, rv, gts, gn, decay_f, decay_b, u1, n1, n2):
    T = rqk.shape[0]
    C = RET_CHUNK
    cps = RET_CHUNKS_PER_STEP
    nc = T // (C * cps)
    lgf = jax.nn.log_sigmoid(decay_f.astype(F32))
    lgb = jax.nn.log_sigmoid(decay_b.astype(F32))
    idx = jnp.arange(C, dtype=F32)
    diff = idx[:, None] - idx[None, :]
    dmat = jnp.where(diff[None] >= 0,
                     jnp.exp(jnp.maximum(diff, 0.0)[None] * lgf[:, None, None]),
                     jnp.exp(jnp.maximum(-diff, 0.0)[None] * lgb[:, None, None]))

    def rows(e):
        return jnp.broadcast_to(e[:, :, None], (RET_HEADS, C, RET_QK_DIM))

    qdf = rows(jnp.exp((idx + 1.0)[None, :] * lgf[:, None]))
    kdf = rows(jnp.exp((C - 1.0 - idx)[None, :] * lgf[:, None]))
    qdb = rows(jnp.exp((C - idx)[None, :] * lgb[:, None]))
    kdb = rows(jnp.exp(idx[None, :] * lgb[:, None]))
    cd = jnp.stack([jnp.exp(C * lgf), jnp.exp(C * lgb)])

    smem = pl.BlockSpec(memory_space=pltpu.SMEM)
    full3 = pl.BlockSpec((RET_HEADS, C, RET_QK_DIM), lambda c: (0, 0, 0))
    state = pltpu.VMEM((RET_HEADS, RET_QK_DIM, RET_V_DIM), F32)

    sb = pl.pallas_call(
        functools.partial(_ret_bwd_kernel, u1=u1, n1=n1, n2=n2),
        grid=(nc,),
        in_specs=[
            smem,
            pl.BlockSpec((cps * C, RET_QK_WIDTH), lambda c: (nc - 1 - c, 1)),
            pl.BlockSpec((cps * C, RET_V_WIDTH), lambda c: (nc - 1 - c, 0)),
            full3,
        ],
        out_specs=pl.BlockSpec((cps, RET_HEADS, RET_QK_DIM, RET_V_DIM), lambda c: (nc - 1 - c, 0, 0, 0)),
        out_shape=jax.ShapeDtypeStruct((nc * cps, RET_HEADS, RET_QK_DIM, RET_V_DIM), BF16),
        scratch_shapes=[state],
        compiler_params=_cparams(("arbitrary",)),
        name="ret_bwd",
    )(cd, rqk, rv, kdb)

    return pl.pallas_call(
        functools.partial(_ret_fwd_kernel, u1=u1, n1=n1, n2=n2),
        grid=(nc,),
        in_specs=[
            smem,
            pl.BlockSpec((cps * C, RET_QK_WIDTH), lambda c: (c, 0)),
            pl.BlockSpec((cps * C, RET_QK_WIDTH), lambda c: (c, 1)),
            pl.BlockSpec((cps * C, RET_V_WIDTH), lambda c: (c, 0)),
            pl.BlockSpec((cps, RET_HEADS, RET_QK_DIM, RET_V_DIM), lambda c: (c, 0, 0, 0)),
            pl.BlockSpec((cps * C, RET_V_WIDTH), lambda c: (c, 0)),
            pl.BlockSpec((1, RET_V_WIDTH), lambda c: (0, 0)),
            pl.BlockSpec((RET_HEADS, C, C), lambda c: (0, 0, 0)),
            full3, full3, full3,
        ],
        out_specs=pl.BlockSpec((cps * C, RET_V_WIDTH), lambda c: (c, 0)),
        out_shape=jax.ShapeDtypeStruct((T, RET_V_WIDTH), BF16),
        scratch_shapes=[state],
        compiler_params=_cparams(("arbitrary",)),
        name="ret_fwd",
    )(cd, rqk, rqk, rv, sb, gts, gn, dmat, qdf, qdb, kdf)


def _merge_kernel(*refs, nx, u1):
    x_refs, a_refs = refs[:nx], refs[nx:nx + 2]
    ro_ref, ga_ref, gr_ref, wa_ref, wr_ref, wo_ref, o_ref = refs[nx + 2:]
    ya = jnp.dot(_read_x(a_refs, u1), wa_ref[...], preferred_element_type=F32)
    yr = jnp.dot(ro_ref[...], wr_ref[...], preferred_element_type=F32)
    mixed = _sigmoid(ga_ref[...].astype(F32)) * ya + _sigmoid(gr_ref[...].astype(F32)) * yr
    o_ref[...] = _read_x(x_refs, u1) + jnp.dot(mixed.astype(BF16), wo_ref[...], preferred_element_type=F32)


def _merge(a12, ro, gts, xs, wa, wr, wo, tm, u1):
    T = ro.shape[0]
    return pl.pallas_call(
        functools.partial(_merge_kernel, nx=len(xs), u1=u1),
        grid=(T // tm,),
        in_specs=_x_specs(xs, tm, u1) + _x_specs(a12, tm, u1) + [
            pl.BlockSpec((tm, RET_V_WIDTH), lambda i: (i, 0)),
            pl.BlockSpec((tm, D_MODEL), lambda i: (i, 1)),
            pl.BlockSpec((tm, D_MODEL), lambda i: (i, 2)),
            pl.BlockSpec((NA_WIDTH, D_MODEL), lambda i: (0, 0)),
            pl.BlockSpec((RET_V_WIDTH, D_MODEL), lambda i: (0, 0)),
            pl.BlockSpec((D_MODEL, D_MODEL), lambda i: (0, 0)),
        ],
        out_specs=pl.BlockSpec((tm, D_MODEL), lambda i: (i, 0)),
        out_shape=jax.ShapeDtypeStruct((T, D_MODEL), F32),
        compiler_params=_cparams(("parallel",)),
        name="merge",
    )(*xs, *a12, ro, gts, gts, wa, wr, wo)


def _ffn_kernel(xp_ref, x_ref, xn_ref, g_ref, wu_ref, cw_ref, wd_ref, gf_ref, *rest, tm, u1, n1, n2, final):
    o_refs, (h_ref, a_ref, xs_ref) = rest[:-3], rest[-3:]
    sub = SUBLANES
    npl = tm // sub
    loc, n = _local(pl.program_id(0), u1, n1, n2)
    g = g_ref[...]

    for q in range(npl // sub):
        piece = jnp.concatenate([x_ref[s * npl + q * sub:s * npl + (q + 1) * sub, :] for s in range(sub)], axis=0)
        piece = jnp.swapaxes(piece.reshape(sub, sub, D_MODEL), 0, 1).reshape(sub * sub, D_MODEL)
        xs_ref[q * sub * sub:(q + 1) * sub * sub, :] = piece
        h_ref[q * sub * sub:(q + 1) * sub * sub, :] = _rms(piece, g).astype(BF16)
    hp = jnp.where(loc == 0, 0.0, _rms(xp_ref[...], g))[sub - 1:sub]
    hn = jnp.where(loc == n - 1, 0.0, _rms(xn_ref[...], g))[0:1]
    hrow = lax.broadcasted_iota(jnp.int32, (2 * sub, D_MODEL), 0)
    h_ref[tm:tm + 2 * sub, :] = jnp.where(hrow == 0, hp, jnp.where(hrow == 1, hn, 0.0)).astype(BF16)

    def up(c0):
        h = h_ref[...]
        return (jnp.dot(h, wu_ref[:, c0:c0 + FFN_TN], preferred_element_type=F32),
                jnp.dot(h, wu_ref[:, D_FF + c0:D_FF + c0 + FFN_TN], preferred_element_type=F32))

    row = lax.broadcasted_iota(jnp.int32, (sub, FFN_TN), 0)

    def conv(u, c0):
        first = jnp.where(row == 0, u[tm:tm + 1], pltpu.roll(u[tm - sub:tm], 1, axis=0))
        last = jnp.where(row == sub - 1, u[tm + 1:tm + 2], pltpu.roll(u[0:sub], sub - 1, axis=0))
        before = jnp.concatenate([first, u[0:tm - sub]], axis=0)
        after = jnp.concatenate([u[sub:tm], last], axis=0)
        return (before * cw_ref[0:1, c0:c0 + FFN_TN] + u[0:tm] * cw_ref[1:2, c0:c0 + FFN_TN]
                + after * cw_ref[2:3, c0:c0 + FFN_TN])

    starts = list(range(0, D_FF, FFN_TN))
    nxt = up(starts[0])
    for t, c0 in enumerate(starts):
        ug, uv = nxt
        if t + 1 < len(starts):
            nxt = up(starts[t + 1])
        a_ref[:, c0:c0 + FFN_TN] = (_gelu_tanh(conv(ug, c0)) * conv(uv, D_FF + c0)).astype(BF16)

    y = jnp.dot(a_ref[...], wd_ref[...], preferred_element_type=F32)
    y = xs_ref[...] + y
    if final:
        y = _rms(y, gf_ref[...])
    y = jnp.swapaxes(y.reshape(npl, sub, D_MODEL), 0, 1).reshape(tm, D_MODEL)
    if len(o_refs) == 1:
        o_refs[0][...] = y
    else:
        @pl.when(pl.program_id(0) < u1)
        def _():
            o_refs[0][...] = y

        @pl.when(pl.program_id(0) >= u1)
        def _():
            o_refs[1][...] = y


def _ffn(x, g, w_up, conv_w, w_down, gf, tm, u1, n1, n2, final):
    T = x.shape[0]
    hb = tm // SUBLANES
    last = T // SUBLANES - 1
    resident = dict(pipeline_mode=pl.Buffered(1))
    if final:
        out_specs = [pl.BlockSpec((tm, D_MODEL), lambda i: (jnp.minimum(i, u1 - 1), 0)),
                     pl.BlockSpec((tm, D_MODEL), lambda i: (jnp.maximum(i - u1, 0), 0))]
        out_shape = [jax.ShapeDtypeStruct((u1 * tm, D_MODEL), F32),
                     jax.ShapeDtypeStruct((T - u1 * tm, D_MODEL), F32)]
    else:
        out_specs = pl.BlockSpec((tm, D_MODEL), lambda i: (i, 0))
        out_shape = jax.ShapeDtypeStruct((T, D_MODEL), F32)
    return pl.pallas_call(
        functools.partial(_ffn_kernel, tm=tm, u1=u1, n1=n1, n2=n2, final=final),
        grid=(T // tm,),
        in_specs=[
            pl.BlockSpec((SUBLANES, D_MODEL), lambda i: (jnp.maximum(i * hb - 1, 0), 0)),
            pl.BlockSpec((tm, D_MODEL), lambda i: (i, 0)),
            pl.BlockSpec((SUBLANES, D_MODEL), lambda i: (jnp.minimum((i + 1) * hb, last), 0)),
            pl.BlockSpec((1, D_MODEL), lambda i: (0, 0)),
            pl.BlockSpec((D_MODEL, 2 * D_FF), lambda i: (0, 0), **resident),
            pl.BlockSpec((3, 2 * D_FF), lambda i: (0, 0), **resident),
            pl.BlockSpec((D_FF, D_MODEL), lambda i: (0, 0), **resident),
            pl.BlockSpec((1, D_MODEL), lambda i: (0, 0)),
        ],
        out_specs=out_specs,
        out_shape=out_shape,
        scratch_shapes=[
            pltpu.VMEM((tm + 2 * SUBLANES, D_MODEL), BF16),
            pltpu.VMEM((tm, D_FF), BF16),
            pltpu.VMEM((tm, D_MODEL), F32),
        ],
        compiler_params=_cparams(("arbitrary",)),
        name="ffn",
    )(x, x, x, g, w_up, conv_w, w_down, gf)


def _rope_tables(seq_len):
    inv_freq = ROPE_BASE ** (-jnp.arange(0, RET_QK_DIM, 2, dtype=F32) / RET_QK_DIM)
    ang = jnp.arange(seq_len, dtype=F32)[:, None] * inv_freq[None, :]
    cos, sin = jnp.cos(ang), jnp.sin(ang)
    return jnp.concatenate([cos, cos], axis=1), jnp.concatenate([-sin, sin], axis=1)


def _pick_tile(limit, *sizes):
    t = limit
    while any(s % t for s in sizes):
        t //= 2
    return t


def kernel(x_prompt, x_sample, norm_mix_g, w_in, na_rel_bias, ret_decay_fwd, ret_decay_bwd, ret_norm_g,
           w_branch_attn, w_branch_ret, w_out, norm_ffn_g, w_up, ffn_conv_w, w_down, norm_final_g):
    B1, S1, D = x_prompt.shape
    B2, S2, _ = x_sample.shape
    T1, T2 = B1 * S1, B2 * S2
    T = T1 + T2
    depth = w_in.shape[0]
    assert D == D_MODEL and w_in.shape[2] == D_IN
    assert S1 % (NA_WH * GRID_W) == 0 and S2 % (NA_WH * GRID_W) == 0 and T1 % S2 == 0

    xs = (x_prompt.reshape(T1, D), x_sample.reshape(T2, D))
    cos2, sin2 = _rope_tables(max(S1, S2))

    tm_proj = _pick_tile(512, S1, S2)
    tm_mix = _pick_tile(512, S1, S2)

    def units(t):
        return T1 // t, S1 // t, S2 // t

    for l in range(depth):
        na, rqk, rv, gts = _inproj(xs, norm_mix_g[l][None].astype(F32), w_in[l].astype(BF16),
                                   cos2, sin2, tm_proj, *units(tm_proj))
        tab = _na_bias_table(na_rel_bias[l])
        a = (_na(na, tab, 0, B1, S1), _na(na, tab, T1, B2, S2))
        ro = _retention(rqk, rv, gts, ret_norm_g[l][None].astype(F32), ret_decay_fwd[l], ret_decay_bwd[l],
                        *units(RET_CHUNK * RET_CHUNKS_PER_STEP))
        x = _merge(a, ro, gts, xs, w_branch_attn[l].astype(BF16), w_branch_ret[l].astype(BF16),
                   w_out[l].astype(BF16), tm_mix, T1 // tm_mix)
        x = _ffn(x, norm_ffn_g[l][None].astype(F32), w_up[l].astype(BF16), ffn_conv_w[l].astype(F32),
                 w_down[l].astype(BF16), norm_final_g[None].astype(F32), tm_mix, *units(tm_mix),
                 final=(l == depth - 1))
        xs = (x,)

    y1, y2 = x
    return y1.reshape(B1, S1, D), y2.reshape(B2, S2, D)
```

```python
import functools

import numpy as np
import jax
import jax.numpy as jnp
from jax import lax
from jax.experimental import pallas as pl
from jax.experimental.pallas import tpu as pltpu

F32 = jnp.float32
BF16 = jnp.bfloat16

D_MODEL = 1024
GRID_W = 64
NA_HEADS = 8
NA_HEAD_DIM = 64
NA_WIDTH = NA_HEADS * NA_HEAD_DIM
NA_WH = 8
NA_WW = 16
RET_HEADS = 4
RET_QK_DIM = 128
RET_V_DIM = 256
RET_QK_WIDTH = RET_HEADS * RET_QK_DIM
RET_V_WIDTH = RET_HEADS * RET_V_DIM
RET_CHUNK = 128
ROPE_BASE = 10000.0
D_FF = 2816
EPS = 1e-6
D_IN = 3 * NA_WIDTH + 2 * RET_QK_WIDTH + 2 * RET_V_WIDTH + 2 * D_MODEL

LANES = 128
SUBLANES = 8
PROJ_TN = 512
FFN_TN = 256
RET_CHUNKS_PER_STEP = 4
NA_ROW_GROUP = 16
MASK_VALUE = -1e30
VMEM_LIMIT = 56 * 1024 * 1024


def _cparams(sem):
    return pltpu.CompilerParams(dimension_semantics=sem, vmem_limit_bytes=VMEM_LIMIT)


def _local(u, u1, n1, n2):
    in1 = u < u1
    loc = jnp.where(in1, lax.rem(u, n1), lax.rem(jnp.maximum(u - u1, 0), n2))
    return loc, jnp.where(in1, n1, n2)


def _rms(x, g):
    ms = jnp.mean(x * x, axis=-1, keepdims=True)
    return (x * lax.rsqrt(ms + EPS)) * g


def _sigmoid(x):
    return 0.5 * (jnp.tanh(0.5 * x) + 1.0)


def _gelu_tanh(x):
    c = np.float32(np.sqrt(2.0 / np.pi))
    return x * (0.5 * (1.0 + jnp.tanh(c * (x + np.float32(0.044715) * (x * x * x)))))


def _x_specs(xs, tm, u1, nblk=None):
    width = xs[0].shape[1]

    def blk(i):
        return i if nblk is None else nblk - 1 - i

    if len(xs) == 1:
        return [pl.BlockSpec((tm, width), lambda i: (blk(i), 0))]
    return [pl.BlockSpec((tm, width), lambda i: (jnp.minimum(blk(i), u1 - 1), 0)),
            pl.BlockSpec((tm, width), lambda i: (jnp.maximum(blk(i) - u1, 0), 0))]


def _read_x(x_refs, u1, blk):
    if len(x_refs) == 1:
        return x_refs[0][...]
    return jnp.where(blk < u1, x_refs[0][...], x_refs[1][...])


def _inproj_kernel(*refs, nx, u1, n1, n2):
    x_refs = refs[:nx]
    (g_ref, w_ref, cos_ref, sin_ref, cd_ref, kdb_ref,
     na_ref, rqk_ref, rv_ref, gts_ref, sb_ref, h_ref, st_ref) = refs[nx:]
    blk = pl.num_programs(0) - 1 - pl.program_id(0)
    loc, n = _local(blk, u1, n1, n2)
    h_ref[...] = _rms(_read_x(x_refs, u1, blk), g_ref[...]).astype(BF16)

    @pl.when(loc == n - 1)
    def _():
        st_ref[...] = jnp.zeros_like(st_ref)

    def proj(c0):
        return jnp.dot(h_ref[...], w_ref[:, c0:c0 + PROJ_TN], preferred_element_type=F32)

    def rope(acc, c0, scale):
        cos2 = cos_ref[...]
        sin2 = sin_ref[...]
        for hd in range(PROJ_TN // RET_QK_DIM):
            xh = acc[:, hd * RET_QK_DIM:(hd + 1) * RET_QK_DIM]
            r = xh * cos2 + pltpu.roll(xh, RET_QK_DIM // 2, axis=1) * sin2
            if scale is not None:
                r = r * scale
            rqk_ref[:, c0 + hd * RET_QK_DIM:c0 + (hd + 1) * RET_QK_DIM] = r

    def sweep_chunk(cc):
        tok = slice(cc * RET_CHUNK, (cc + 1) * RET_CHUNK)
        for h in range(RET_HEADS):
            st = st_ref[h]
            sb_ref[cc, h] = st.astype(BF16)
            k = rqk_ref[tok, RET_QK_WIDTH + h * RET_QK_DIM:RET_QK_WIDTH + (h + 1) * RET_QK_DIM]
            vh = rv_ref[tok, h * RET_V_DIM:(h + 1) * RET_V_DIM]
            st_ref[h] = st * cd_ref[1, h] + lax.dot_general(
                (k * kdb_ref[h]).astype(BF16), vh, (((0,), (0,)), ((), ())), preferred_element_type=F32)

    c_na, c_rq = 0, 3 * NA_WIDTH
    c_rk, c_rv = c_rq + RET_QK_WIDTH, c_rq + 2 * RET_QK_WIDTH
    c_gts = c_rv + RET_V_WIDTH
    rope(proj(c_rk), RET_QK_WIDTH, np.float32(RET_QK_DIM ** -0.5))
    for c0 in range(0, RET_V_WIDTH, PROJ_TN):
        rv_ref[:, c0:c0 + PROJ_TN] = proj(c_rv + c0).astype(BF16)
    pending = list(reversed(range(RET_CHUNKS_PER_STEP)))
    for c0 in range(0, 3 * NA_WIDTH, PROJ_TN):
        na_ref[:, c0:c0 + PROJ_TN] = proj(c_na + c0).astype(BF16)
        if pending:
            sweep_chunk(pending.pop(0))
    rope(proj(c_rq), 0, None)
    for c0 in range(0, RET_V_WIDTH + 2 * D_MODEL, PROJ_TN):
        if pending:
            sweep_chunk(pending.pop(0))
        gts_ref[:, c0:c0 + PROJ_TN] = proj(c_gts + c0).astype(BF16)


def _inproj(xs, g, w, cos2, sin2, cd, kdb, tm, u1, n1, n2):
    T = sum(x.shape[0] for x in xs)
    nblk = T // tm
    cps = RET_CHUNKS_PER_STEP
    assert RET_QK_WIDTH == PROJ_TN and tm == cps * RET_CHUNK

    def tab_map(i):
        return (_local(nblk - 1 - i, u1, n1, n2)[0], 0)

    def rows(width):
        return pl.BlockSpec((tm, width), lambda i: (nblk - 1 - i, 0))

    return pl.pallas_call(
        functools.partial(_inproj_kernel, nx=len(xs), u1=u1, n1=n1, n2=n2),
        grid=(nblk,),
        in_specs=_x_specs(xs, tm, u1, nblk) + [
            pl.BlockSpec((1, D_MODEL), lambda i: (0, 0)),
            pl.BlockSpec((D_MODEL, D_IN), lambda i: (0, 0), pipeline_mode=pl.Buffered(1)),
            pl.BlockSpec((tm, RET_QK_DIM), tab_map),
            pl.BlockSpec((tm, RET_QK_DIM), tab_map),
            pl.BlockSpec(memory_space=pltpu.SMEM),
            pl.BlockSpec((RET_HEADS, RET_CHUNK, RET_QK_DIM), lambda i: (0, 0, 0)),
        ],
        out_specs=[
            rows(3 * NA_WIDTH), rows(2 * RET_QK_WIDTH), rows(RET_V_WIDTH), rows(RET_V_WIDTH + 2 * D_MODEL),
            pl.BlockSpec((cps, RET_HEADS, RET_QK_DIM, RET_V_DIM), lambda i: (nblk - 1 - i, 0, 0, 0)),
        ],
        out_shape=[
            jax.ShapeDtypeStruct((T, 3 * NA_WIDTH), BF16),
            jax.ShapeDtypeStruct((T, 2 * RET_QK_WIDTH), F32),
            jax.ShapeDtypeStruct((T, RET_V_WIDTH), BF16),
            jax.ShapeDtypeStruct((T, RET_V_WIDTH + 2 * D_MODEL), BF16),
            jax.ShapeDtypeStruct((nblk * cps, RET_HEADS, RET_QK_DIM, RET_V_DIM), BF16),
        ],
        scratch_shapes=[pltpu.VMEM((tm, D_MODEL), BF16),
                        pltpu.VMEM((RET_HEADS, RET_QK_DIM, RET_V_DIM), F32)],
        compiler_params=_cparams(("arbitrary",)),
        name="inproj",
    )(*xs, g, w, cos2, sin2, cd, kdb)


def _na_kernel(q_ref, k_ref, v_ref, tab_ref, o_ref, *, rows):
    pair = 2 * GRID_W
    lane = lax.broadcasted_iota(jnp.int32, (pair, LANES), 1)
    sub = lax.broadcasted_iota(jnp.int32, (pair, LANES), 0)
    keep = (lane < NA_HEAD_DIM) == (sub < GRID_W)
    low = lax.broadcasted_iota(jnp.int32, (SUBLANES, LANES), 1) < NA_HEAD_DIM
    scale = NA_HEAD_DIM ** -0.5

    def tok(row):
        return pl.ds(pl.multiple_of(row * GRID_W, GRID_W), GRID_W)

    ncb = GRID_W // NA_WW
    qcol = np.arange(GRID_W)
    qstart = np.clip(qcol - NA_WW // 2, 0, GRID_W - NA_WW)
    live = []
    for i in range(GRID_W // SUBLANES):
        lo, hi = qstart[i * SUBLANES], qstart[(i + 1) * SUBLANES - 1] + NA_WW
        live.append([j for j in range(ncb) if j * NA_WW < hi and (j + 1) * NA_WW > lo])

    def band_rows(ref, rs):
        rows_ = ref[pl.ds(pl.multiple_of(rs * GRID_W, GRID_W), NA_WH * GRID_W), :]
        return jnp.concatenate(
            [rows_[w * GRID_W + j * NA_WW:w * GRID_W + (j + 1) * NA_WW]
             for j in range(ncb) for w in range(NA_WH)], axis=0)

    def scores(r):
        rs = jnp.clip(r - NA_WH // 2, 0, rows - NA_WH)
        q = q_ref[tok(r), :] * scale
        qm = jnp.where(keep, jnp.concatenate([q, q], axis=0), jnp.zeros((pair, LANES), BF16))
        s = lax.dot_general(qm, band_rows(k_ref, rs), (((1,), (1,)), ((), ())), preferred_element_type=F32)
        d = r - rs
        tiles = {}
        for b in range(pair // SUBLANES):
            for j in live[b % (GRID_W // SUBLANES)]:
                rows8, lanes = slice(b * SUBLANES, (b + 1) * SUBLANES), slice(j * LANES, (j + 1) * LANES)
                tiles[b, j] = s[rows8, lanes] + tab_ref[0, d, rows8, lanes]
        return tiles

    def softmax(tiles):
        p, ls = {}, []
        for b in range(pair // SUBLANES):
            mine = [tiles[b, j] for j in live[b % (GRID_W // SUBLANES)]]
            m = jnp.max(functools.reduce(jnp.maximum, mine), axis=-1, keepdims=True)
            es = [jnp.exp(t - m) for t in mine]
            for j, e in zip(live[b % (GRID_W // SUBLANES)], es):
                p[b, j] = e
            ls.append(jnp.sum(functools.reduce(jnp.add, es), axis=-1, keepdims=True))
        zero = jnp.zeros((SUBLANES, LANES), F32)
        full = jnp.concatenate(
            [jnp.concatenate([p.get((b, j), zero) for j in range(ncb)], axis=1)
             for b in range(pair // SUBLANES)], axis=0)
        return full.astype(BF16), ls

    def weighted(r, p, ls):
        rs = jnp.clip(r - NA_WH // 2, 0, rows - NA_WH)
        o = jnp.dot(p, band_rows(v_ref, rs), preferred_element_type=F32)
        nb = GRID_W // SUBLANES
        out = [jnp.where(low,
                         o[i * SUBLANES:(i + 1) * SUBLANES] / ls[i],
                         o[(nb + i) * SUBLANES:(nb + i + 1) * SUBLANES] / ls[nb + i]) for i in range(nb)]
        o_ref[tok(r), :] = jnp.concatenate(out, axis=0).astype(BF16)

    def body(g, carry):
        rs_ = [g * NA_ROW_GROUP + i for i in range(NA_ROW_GROUP)]
        ss = [scores(r) for r in rs_]
        pls = [softmax(s) for s in ss]
        for r, (p, l) in zip(rs_, pls):
            weighted(r, p, l)
        return carry

    lax.fori_loop(0, rows // NA_ROW_GROUP, body, 0)


def _na(na, tab, tok0, nseq, S):
    rows = S // GRID_W
    b0 = tok0 // S
    npair = NA_HEADS // 2
    return pl.pallas_call(
        functools.partial(_na_kernel, rows=rows),
        grid=(nseq, npair),
        in_specs=[
            pl.BlockSpec((S, LANES), lambda b, hp: (b0 + b, hp)),
            pl.BlockSpec((S, LANES), lambda b, hp: (b0 + b, npair + hp)),
            pl.BlockSpec((S, LANES), lambda b, hp: (b0 + b, 2 * npair + hp)),
            pl.BlockSpec((1, NA_WH, 2 * GRID_W, NA_WH * GRID_W), lambda b, hp: (hp, 0, 0, 0)),
        ],
        out_specs=pl.BlockSpec((S, LANES), lambda b, hp: (b, hp)),
        out_shape=jax.ShapeDtypeStruct((nseq * S, NA_WIDTH), BF16),
        compiler_params=_cparams(("parallel", "parallel")),
        name="na",
    )(na, na, na, tab)


def _na_bias_table(rel_bias):
    H = rel_bias.shape[0]
    col = np.arange(GRID_W)
    col_start = np.clip(col - NA_WW // 2, 0, GRID_W - NA_WW)
    valid = (col[None, :] >= col_start[:, None]) & (col[None, :] < col_start[:, None] + NA_WW)
    rel = col[None, :] - col[:, None] + (NA_WW - 1)
    pick = (valid[:, :, None] & (rel[:, :, None] == np.arange(2 * NA_WW - 1))).astype(np.float32)
    cm = jnp.einsum('hij,ckj->hick', rel_bias.astype(F32), pick, precision=lax.Precision.HIGHEST)
    cm = jnp.where(valid[None, None], cm, MASK_VALUE)
    per_d = [jnp.transpose(cm[:, NA_WH - 1 - d:2 * NA_WH - 1 - d], (0, 2, 1, 3)) for d in range(NA_WH)]
    t = jnp.stack(per_d, axis=1)
    t = t.reshape(H, NA_WH, GRID_W, NA_WH, GRID_W // NA_WW, NA_WW).transpose(0, 1, 2, 4, 3, 5)
    t = t.reshape(H // 2, 2, NA_WH, GRID_W, NA_WH * GRID_W)
    return jnp.transpose(t, (0, 2, 1, 3, 4)).reshape(H // 2, NA_WH, 2 * GRID_W, NA_WH * GRID_W)


def _mix_kernel(*refs, nx, u1, n1, n2):
    x_refs, a_refs = refs[:nx], refs[nx:nx + 2]
    (cd_ref, q_ref, k_ref, v_ref, sb_ref, rg_ref, ga_ref, gr_ref, gn_ref, dm_ref, qdf_ref, qdb_ref, kdf_ref,
     wa_ref, wr_ref, wo_ref, x_out_ref, st_ref, o_ref) = refs[nx + 2:]
    blk = pl.program_id(0)
    loc, _ = _local(blk, u1, n1, n2)

    @pl.when(loc == 0)
    def _():
        st_ref[...] = jnp.zeros_like(st_ref)

    ya = jnp.dot(_read_x(a_refs, u1, blk), wa_ref[...], preferred_element_type=F32)

    for cc in range(RET_CHUNKS_PER_STEP):
        tok = slice(cc * RET_CHUNK, (cc + 1) * RET_CHUNK)
        for h in range(RET_HEADS):
            qh = q_ref[tok, h * RET_QK_DIM:(h + 1) * RET_QK_DIM]
            kh = k_ref[tok, h * RET_QK_DIM:(h + 1) * RET_QK_DIM]
            vh = v_ref[tok, h * RET_V_DIM:(h + 1) * RET_V_DIM]
            st = st_ref[h]
            s = lax.dot_general(qh.astype(BF16), kh.astype(BF16), (((1,), (1,)), ((), ())),
                                preferred_element_type=F32)
            o = jnp.dot((s * dm_ref[h]).astype(BF16), vh, preferred_element_type=F32)
            o = o + jnp.dot((qh * qdf_ref[h]).astype(BF16), st.astype(BF16), preferred_element_type=F32)
            o = o + jnp.dot((qh * qdb_ref[h]).astype(BF16), sb_ref[cc, h], preferred_element_type=F32)
            st_ref[h] = st * cd_ref[0, h] + lax.dot_general(
                (kh * kdf_ref[h]).astype(BF16), vh, (((0,), (0,)), ((), ())), preferred_element_type=F32)
            mu = jnp.mean(o, axis=-1, keepdims=True)
            oc = o - mu
            var = jnp.mean(oc * oc, axis=-1, keepdims=True)
            y = (oc * lax.rsqrt(var + EPS)) * gn_ref[:, h * RET_V_DIM:(h + 1) * RET_V_DIM]
            rg = rg_ref[tok, h * RET_V_DIM:(h + 1) * RET_V_DIM].astype(F32)
            o_ref[tok, h * RET_V_DIM:(h + 1) * RET_V_DIM] = ((rg * _sigmoid(rg)) * y).astype(BF16)

    yr = jnp.dot(o_ref[...], wr_ref[...], preferred_element_type=F32)
    mixed = _sigmoid(ga_ref[...].astype(F32)) * ya + _sigmoid(gr_ref[...].astype(F32)) * yr
    x_out_ref[...] = _read_x(x_refs, u1, blk) + jnp.dot(mixed.astype(BF16), wo_ref[...],
                                                       preferred_element_type=F32)


def _ret_tables(decay_f, decay_b):
    C = RET_CHUNK
    lgf = jax.nn.log_sigmoid(decay_f.astype(F32))
    lgb = jax.nn.log_sigmoid(decay_b.astype(F32))
    idx = jnp.arange(C, dtype=F32)
    diff = idx[:, None] - idx[None, :]
    dmat = jnp.where(diff[None] >= 0,
                     jnp.exp(jnp.maximum(diff, 0.0)[None] * lgf[:, None, None]),
                     jnp.exp(jnp.maximum(-diff, 0.0)[None] * lgb[:, None, None]))

    def rows(e):
        return jnp.broadcast_to(e[:, :, None], (RET_HEADS, C, RET_QK_DIM))

    qdf = rows(jnp.exp((idx + 1.0)[None, :] * lgf[:, None]))
    kdf = rows(jnp.exp((C - 1.0 - idx)[None, :] * lgf[:, None]))
    qdb = rows(jnp.exp((C - idx)[None, :] * lgb[:, None]))
    kdb = rows(jnp.exp(idx[None, :] * lgb[:, None]))
    cd = jnp.stack([jnp.exp(C * lgf), jnp.exp(C * lgb)])
    return dict(dmat=dmat, qdf=qdf, kdf=kdf, qdb=qdb, kdb=kdb, cd=cd)


def _mix(xs, a12, rqk, rv, sb, gts, gn, tabs, wa, wr, wo, u1, n1, n2):
    T = rqk.shape[0]
    C = RET_CHUNK
    tm = RET_CHUNKS_PER_STEP * C
    smem = pl.BlockSpec(memory_space=pltpu.SMEM)
    full3 = pl.BlockSpec((RET_HEADS, C, RET_QK_DIM), lambda i: (0, 0, 0))

    def rows(width, col=0):
        return pl.BlockSpec((tm, width), lambda i: (i, col))

    return pl.pallas_call(
        functools.partial(_mix_kernel, nx=len(xs), u1=u1, n1=n1, n2=n2),
        grid=(T // tm,),
        in_specs=_x_specs(xs, tm, u1) + _x_specs(a12, tm, u1) + [
            smem,
            rows(RET_QK_WIDTH, 0), rows(RET_QK_WIDTH, 1), rows(RET_V_WIDTH),
            pl.BlockSpec((RET_CHUNKS_PER_STEP, RET_HEADS, RET_QK_DIM, RET_V_DIM), lambda i: (i, 0, 0, 0)),
            rows(RET_V_WIDTH, 0), rows(D_MODEL, 1), rows(D_MODEL, 2),
            pl.BlockSpec((1, RET_V_WIDTH), lambda i: (0, 0)),
            pl.BlockSpec((RET_HEADS, C, C), lambda i: (0, 0, 0)),
            full3, full3, full3,
            pl.BlockSpec((NA_WIDTH, D_MODEL), lambda i: (0, 0)),
            pl.BlockSpec((RET_V_WIDTH, D_MODEL), lambda i: (0, 0)),
            pl.BlockSpec((D_MODEL, D_MODEL), lambda i: (0, 0)),
        ],
        out_specs=rows(D_MODEL),
        out_shape=jax.ShapeDtypeStruct((T, D_MODEL), F32),
        scratch_shapes=[pltpu.VMEM((RET_HEADS, RET_QK_DIM, RET_V_DIM), F32),
                        pltpu.VMEM((tm, RET_V_WIDTH), BF16)],
        compiler_params=_cparams(("arbitrary",)),
        name="mix",
    )(*xs, *a12, tabs["cd"], rqk, rqk, rv, sb, gts, gts, gts, gn,
      tabs["dmat"], tabs["qdf"], tabs["qdb"], tabs["kdf"], wa, wr, wo)


def _ffn_kernel(xp_ref, x_ref, xn_ref, g_ref, wu_ref, cw_ref, wd_ref, gf_ref, *rest, tm, u1, n1, n2, final):
    o_refs, (h_ref, a_ref, xs_ref) = rest[:-3], rest[-3:]
    sub = SUBLANES
    npl = tm // sub
    loc, n = _local(pl.program_id(0), u1, n1, n2)
    g = g_ref[...]

    xs_ref[...] = jnp.swapaxes(x_ref[...].reshape(sub, npl, D_MODEL), 0, 1).reshape(tm, D_MODEL)
    h_ref[0:tm, :] = _rms(xs_ref[...], g).astype(BF16)
    hp = jnp.where(loc == 0, 0.0, _rms(xp_ref[...], g))[sub - 1:sub]
    hn = jnp.where(loc == n - 1, 0.0, _rms(xn_ref[...], g))[0:1]
    hrow = lax.broadcasted_iota(jnp.int32, (2 * sub, D_MODEL), 0)
    h_ref[tm:tm + 2 * sub, :] = jnp.where(hrow == 0, hp, jnp.where(hrow == 1, hn, 0.0)).astype(BF16)

    def up(c0):
        h = h_ref[...]
        return (jnp.dot(h, wu_ref[:, c0:c0 + FFN_TN], preferred_element_type=F32),
                jnp.dot(h, wu_ref[:, D_FF + c0:D_FF + c0 + FFN_TN], preferred_element_type=F32))

    row = lax.broadcasted_iota(jnp.int32, (sub, FFN_TN), 0)

    def conv(u, c0):
        first = jnp.where(row == 0, u[tm:tm + 1], pltpu.roll(u[tm - sub:tm], 1, axis=0))
        last = jnp.where(row == sub - 1, u[tm + 1:tm + 2], pltpu.roll(u[0:sub], sub - 1, axis=0))
        before = jnp.concatenate([first, u[0:tm - sub]], axis=0)
        after = jnp.concatenate([u[sub:tm], last], axis=0)
        return (before * cw_ref[0:1, c0:c0 + FFN_TN] + u[0:tm] * cw_ref[1:2, c0:c0 + FFN_TN]
                + after * cw_ref[2:3, c0:c0 + FFN_TN])

    starts = list(range(0, D_FF, FFN_TN))
    nxt = up(starts[0])
    for t, c0 in enumerate(starts):
        ug, uv = nxt
        if t + 1 < len(starts):
            nxt = up(starts[t + 1])
        a_ref[:, c0:c0 + FFN_TN] = (_gelu_tanh(conv(ug, c0)) * conv(uv, D_FF + c0)).astype(BF16)

    y = jnp.dot(a_ref[...], wd_ref[...], preferred_element_type=F32)
    y = xs_ref[...] + y
    if final:
        y = _rms(y, gf_ref[...])
    y = jnp.swapaxes(y.reshape(npl, sub, D_MODEL), 0, 1).reshape(tm, D_MODEL)
    if len(o_refs) == 1:
        o_refs[0][...] = y
    else:
        @pl.when(pl.program_id(0) < u1)
        def _():
            o_refs[0][...] = y

        @pl.when(pl.program_id(0) >= u1)
        def _():
            o_refs[1][...] = y


def _ffn(x, g, w_up, conv_w, w_down, gf, tm, u1, n1, n2, final):
    T = x.shape[0]
    hb = tm // SUBLANES
    last = T // SUBLANES - 1
    resident = dict(pipeline_mode=pl.Buffered(1))
    if final:
        out_specs = [pl.BlockSpec((tm, D_MODEL), lambda i: (jnp.minimum(i, u1 - 1), 0)),
                     pl.BlockSpec((tm, D_MODEL), lambda i: (jnp.maximum(i - u1, 0), 0))]
        out_shape = [jax.ShapeDtypeStruct((u1 * tm, D_MODEL), F32),
                     jax.ShapeDtypeStruct((T - u1 * tm, D_MODEL), F32)]
    else:
        out_specs = pl.BlockSpec((tm, D_MODEL), lambda i: (i, 0))
        out_shape = jax.ShapeDtypeStruct((T, D_MODEL), F32)
    return pl.pallas_call(
        functools.partial(_ffn_kernel, tm=tm, u1=u1, n1=n1, n2=n2, final=final),
        grid=(T // tm,),
        in_specs=[
            pl.BlockSpec((SUBLANES, D_MODEL), lambda i: (jnp.maximum(i * hb - 1, 0), 0)),
            pl.BlockSpec((tm, D_MODEL), lambda i: (i, 0)),
            pl.BlockSpec((SUBLANES, D_MODEL), lambda i: (jnp.minimum((i + 1) * hb, last), 0)),
            pl.BlockSpec((1, D_MODEL), lambda i: (0, 0)),
            pl.BlockSpec((D_MODEL, 2 * D_FF), lambda i: (0, 0), **resident),
            pl.BlockSpec((3, 2 * D_FF), lambda i: (0, 0), **resident),
            pl.BlockSpec((D_FF, D_MODEL), lambda i: (0, 0), **resident),
            pl.BlockSpec((1, D_MODEL), lambda i: (0, 0)),
        ],
        out_specs=out_specs,
        out_shape=out_shape,
        scratch_shapes=[
            pltpu.VMEM((tm + 2 * SUBLANES, D_MODEL), BF16),
            pltpu.VMEM((tm, D_FF), BF16),
            pltpu.VMEM((tm, D_MODEL), F32),
        ],
        compiler_params=_cparams(("arbitrary",)),
        name="ffn",
    )(x, x, x, g, w_up, conv_w, w_down, gf)


def _rope_tables(seq_len):
    inv_freq = ROPE_BASE ** (-jnp.arange(0, RET_QK_DIM, 2, dtype=F32) / RET_QK_DIM)
    ang = jnp.arange(seq_len, dtype=F32)[:, None] * inv_freq[None, :]
    cos, sin = jnp.cos(ang), jnp.sin(ang)
    return jnp.concatenate([cos, cos], axis=1), jnp.concatenate([-sin, sin], axis=1)


def _pick_tile(limit, *sizes):
    t = limit
    while any(s % t for s in sizes):
        t //= 2
    return t


def kernel(x_prompt, x_sample, norm_mix_g, w_in, na_rel_bias, ret_decay_fwd, ret_decay_bwd, ret_norm_g,
           w_branch_attn, w_branch_ret, w_out, norm_ffn_g, w_up, ffn_conv_w, w_down, norm_final_g):
    B1, S1, D = x_prompt.shape
    B2, S2, _ = x_sample.shape
    T1, T2 = B1 * S1, B2 * S2
    T = T1 + T2
    depth = w_in.shape[0]
    assert D == D_MODEL and w_in.shape[2] == D_IN
    assert S1 % (NA_WH * GRID_W) == 0 and S2 % (NA_WH * GRID_W) == 0 and T1 % S2 == 0

    xs = (x_prompt.reshape(T1, D), x_sample.reshape(T2, D))
    cos2, sin2 = _rope_tables(max(S1, S2))

    tm_proj = _pick_tile(512, S1, S2)
    tm_mix = _pick_tile(512, S1, S2)

    def units(t):
        return T1 // t, S1 // t, S2 // t

    for l in range(depth):
        tabs = _ret_tables(ret_decay_fwd[l], ret_decay_bwd[l])
        na, rqk, rv, gts, sb = _inproj(xs, norm_mix_g[l][None].astype(F32), w_in[l].astype(BF16),
                                       cos2, sin2, tabs["cd"], tabs["kdb"], tm_proj, *units(tm_proj))
        tab = _na_bias_table(na_rel_bias[l])
        a = (_na(na, tab, 0, B1, S1), _na(na, tab, T1, B2, S2))
        x = _mix(xs, a, rqk, rv, sb, gts, ret_norm_g[l][None].astype(F32), tabs,
                 w_branch_attn[l].astype(BF16), w_branch_ret[l].astype(BF16), w_out[l].astype(BF16),
                 *units(RET_CHUNK * RET_CHUNKS_PER_STEP))
        x = _ffn(x, norm_ffn_g[l][None].astype(F32), w_up[l].astype(BF16), ffn_conv_w[l].astype(F32),
                 w_down[l].astype(BF16), norm_final_g[None].astype(F32), tm_mix, *units(tm_mix),
                 final=(l == depth - 1))
        xs = (x,)

    y1, y2 = x
    return y1.reshape(B1, S1, D), y2.reshape(B2, S2, D)
```

```python
import functools

import numpy as np
import jax
import jax.numpy as jnp
from jax import lax
from jax.experimental import pallas as pl
from jax.experimental.pallas import tpu as pltpu

F32 = jnp.float32
BF16 = jnp.bfloat16

D_MODEL = 1024
GRID_W = 64
NA_HEADS = 8
NA_HEAD_DIM = 64
NA_WIDTH = NA_HEADS * NA_HEAD_DIM
NA_WH = 8
NA_WW = 16
RET_HEADS = 4
RET_QK_DIM = 128
RET_V_DIM = 256
RET_QK_WIDTH = RET_HEADS * RET_QK_DIM
RET_V_WIDTH = RET_HEADS * RET_V_DIM
RET_CHUNK = 128
ROPE_BASE = 10000.0
D_FF = 2816
EPS = 1e-6
D_IN = 3 * NA_WIDTH + 2 * RET_QK_WIDTH + 2 * RET_V_WIDTH + 2 * D_MODEL

LANES = 128
SUBLANES = 8
PROJ_TN = 512
FFN_TN = 256
RET_CHUNKS_PER_STEP = 4
NA_ROW_GROUP = 16
MASK_VALUE = -1e30
VMEM_LIMIT = 56 * 1024 * 1024


def _cparams(sem):
    return pltpu.CompilerParams(dimension_semantics=sem, vmem_limit_bytes=VMEM_LIMIT)


def _layer_spec(shape, layer, **kwargs):
    return pl.BlockSpec((None,) + tuple(shape), lambda *_: (layer,) + (0,) * len(shape), **kwargs)


def _local(u, u1, n1, n2):
    in1 = u < u1
    loc = jnp.where(in1, lax.rem(u, n1), lax.rem(jnp.maximum(u - u1, 0), n2))
    return loc, jnp.where(in1, n1, n2)


def _rms(x, g):
    ms = jnp.mean(x * x, axis=-1, keepdims=True)
    return (x * lax.rsqrt(ms + EPS)) * g


def _sigmoid(x):
    return 0.5 * (jnp.tanh(0.5 * x) + 1.0)


def _gelu_tanh(x):
    c = np.float32(np.sqrt(2.0 / np.pi))
    return x * (0.5 * (1.0 + jnp.tanh(c * (x + np.float32(0.044715) * (x * x * x)))))


def _x_specs(xs, tm, u1, nblk=None):
    width = xs[0].shape[1]

    def blk(i):
        return i if nblk is None else nblk - 1 - i

    if len(xs) == 1:
        return [pl.BlockSpec((tm, width), lambda i: (blk(i), 0))]
    return [pl.BlockSpec((tm, width), lambda i: (jnp.minimum(blk(i), u1 - 1), 0)),
            pl.BlockSpec((tm, width), lambda i: (jnp.maximum(blk(i) - u1, 0), 0))]


def _read_x(x_refs, u1, blk):
    if len(x_refs) == 1:
        return x_refs[0][...]
    return jnp.where(blk < u1, x_refs[0][...], x_refs[1][...])


def _inproj_kernel(*refs, nx, u1, n1, n2):
    x_refs = refs[:nx]
    (g_ref, w_ref, cos_ref, sin_ref, cd_ref, kdb_ref,
     na_ref, rqk_ref, rv_ref, gts_ref, sb_ref, h_ref, st_ref) = refs[nx:]
    blk = pl.num_programs(0) - 1 - pl.program_id(0)
    loc, n = _local(blk, u1, n1, n2)
    h_ref[...] = _rms(_read_x(x_refs, u1, blk), g_ref[...]).astype(BF16)

    @pl.when(loc == n - 1)
    def _():
        st_ref[...] = jnp.zeros_like(st_ref)

    def proj(c0):
        return jnp.dot(h_ref[...], w_ref[:, c0:c0 + PROJ_TN], preferred_element_type=F32)

    def rope(acc, c0, scale):
        cos2 = cos_ref[...]
        sin2 = sin_ref[...]
        for hd in range(PROJ_TN // RET_QK_DIM):
            xh = acc[:, hd * RET_QK_DIM:(hd + 1) * RET_QK_DIM]
            r = xh * cos2 + pltpu.roll(xh, RET_QK_DIM // 2, axis=1) * sin2
            if scale is not None:
                r = r * scale
            rqk_ref[:, c0 + hd * RET_QK_DIM:c0 + (hd + 1) * RET_QK_DIM] = r

    def sweep_chunk(cc):
        tok = slice(cc * RET_CHUNK, (cc + 1) * RET_CHUNK)
        for h in range(RET_HEADS):
            st = st_ref[h]
            sb_ref[cc, h] = st.astype(BF16)
            k = rqk_ref[tok, RET_QK_WIDTH + h * RET_QK_DIM:RET_QK_WIDTH + (h + 1) * RET_QK_DIM]
            vh = rv_ref[tok, h * RET_V_DIM:(h + 1) * RET_V_DIM]
            st_ref[h] = st * cd_ref[1, h] + lax.dot_general(
                (k * kdb_ref[h]).astype(BF16), vh, (((0,), (0,)), ((), ())), preferred_element_type=F32)

    c_na, c_rq = 0, 3 * NA_WIDTH
    c_rk, c_rv = c_rq + RET_QK_WIDTH, c_rq + 2 * RET_QK_WIDTH
    c_gts = c_rv + RET_V_WIDTH
    rope(proj(c_rk), RET_QK_WIDTH, np.float32(RET_QK_DIM ** -0.5))
    for c0 in range(0, RET_V_WIDTH, PROJ_TN):
        rv_ref[:, c0:c0 + PROJ_TN] = proj(c_rv + c0).astype(BF16)
    pending = list(reversed(range(RET_CHUNKS_PER_STEP)))
    for c0 in range(0, 3 * NA_WIDTH, PROJ_TN):
        na_ref[:, c0:c0 + PROJ_TN] = proj(c_na + c0).astype(BF16)
        if pending:
            sweep_chunk(pending.pop(0))
    rope(proj(c_rq), 0, None)
    for c0 in range(0, RET_V_WIDTH + 2 * D_MODEL, PROJ_TN):
        if pending:
            sweep_chunk(pending.pop(0))
        gts_ref[:, c0:c0 + PROJ_TN] = proj(c_gts + c0).astype(BF16)


def _inproj(xs, layer, g, w, cos2, sin2, cd, kdb, tm, u1, n1, n2):
    T = sum(x.shape[0] for x in xs)
    nblk = T // tm
    cps = RET_CHUNKS_PER_STEP
    assert RET_QK_WIDTH == PROJ_TN and tm == cps * RET_CHUNK

    def tab_map(i):
        return (_local(nblk - 1 - i, u1, n1, n2)[0], 0)

    def rows(width):
        return pl.BlockSpec((tm, width), lambda i: (nblk - 1 - i, 0))

    return pl.pallas_call(
        functools.partial(_inproj_kernel, nx=len(xs), u1=u1, n1=n1, n2=n2),
        grid=(nblk,),
        in_specs=_x_specs(xs, tm, u1, nblk) + [
            _layer_spec((1, D_MODEL), layer),
            _layer_spec((D_MODEL, D_IN), layer, pipeline_mode=pl.Buffered(1)),
            pl.BlockSpec((tm, RET_QK_DIM), tab_map),
            pl.BlockSpec((tm, RET_QK_DIM), tab_map),
            pl.BlockSpec(memory_space=pltpu.SMEM),
            _layer_spec((RET_HEADS, RET_CHUNK, RET_QK_DIM), layer),
        ],
        out_specs=[
            rows(3 * NA_WIDTH), rows(2 * RET_QK_WIDTH), rows(RET_V_WIDTH), rows(RET_V_WIDTH + 2 * D_MODEL),
            pl.BlockSpec((cps, RET_HEADS, RET_QK_DIM, RET_V_DIM), lambda i: (nblk - 1 - i, 0, 0, 0)),
        ],
        out_shape=[
            jax.ShapeDtypeStruct((T, 3 * NA_WIDTH), BF16),
            jax.ShapeDtypeStruct((T, 2 * RET_QK_WIDTH), F32),
            jax.ShapeDtypeStruct((T, RET_V_WIDTH), BF16),
            jax.ShapeDtypeStruct((T, RET_V_WIDTH + 2 * D_MODEL), BF16),
            jax.ShapeDtypeStruct((nblk * cps, RET_HEADS, RET_QK_DIM, RET_V_DIM), BF16),
        ],
        scratch_shapes=[pltpu.VMEM((tm, D_MODEL), BF16),
                        pltpu.VMEM((RET_HEADS, RET_QK_DIM, RET_V_DIM), F32)],
        compiler_params=_cparams(("arbitrary",)),
        name="inproj",
    )(*xs, g, w, cos2, sin2, cd, kdb)


def _na_kernel(q_ref, k_ref, v_ref, tab_ref, o_ref, *, rows):
    pair = 2 * GRID_W
    lane = lax.broadcasted_iota(jnp.int32, (pair, LANES), 1)
    sub = lax.broadcasted_iota(jnp.int32, (pair, LANES), 0)
    keep = (lane < NA_HEAD_DIM) == (sub < GRID_W)
    low = lax.broadcasted_iota(jnp.int32, (SUBLANES, LANES), 1) < NA_HEAD_DIM
    scale = NA_HEAD_DIM ** -0.5

    def tok(row):
        return pl.ds(pl.multiple_of(row * GRID_W, GRID_W), GRID_W)

    ncb = GRID_W // NA_WW
    qcol = np.arange(GRID_W)
    qstart = np.clip(qcol - NA_WW // 2, 0, GRID_W - NA_WW)
    live = []
    for i in range(GRID_W // SUBLANES):
        lo, hi = qstart[i * SUBLANES], qstart[(i + 1) * SUBLANES - 1] + NA_WW
        live.append([j for j in range(ncb) if j * NA_WW < hi and (j + 1) * NA_WW > lo])

    def band_rows(ref, rs):
        rows_ = ref[pl.ds(pl.multiple_of(rs * GRID_W, GRID_W), NA_WH * GRID_W), :]
        return jnp.concatenate(
            [rows_[w * GRID_W + j * NA_WW:w * GRID_W + (j + 1) * NA_WW]
             for j in range(ncb) for w in range(NA_WH)], axis=0)

    def scores(r):
        rs = jnp.clip(r - NA_WH // 2, 0, rows - NA_WH)
        q = q_ref[tok(r), :] * scale
        qm = jnp.where(keep, jnp.concatenate([q, q], axis=0), jnp.zeros((pair, LANES), BF16))
        s = lax.dot_general(qm, band_rows(k_ref, rs), (((1,), (1,)), ((), ())), preferred_element_type=F32)
        d = r - rs
        tiles = {}
        for b in range(pair // SUBLANES):
            for j in live[b % (GRID_W // SUBLANES)]:
                rows8, lanes = slice(b * SUBLANES, (b + 1) * SUBLANES), slice(j * LANES, (j + 1) * LANES)
                tiles[b, j] = s[rows8, lanes] + tab_ref[0, d, rows8, lanes]
        return tiles

    def softmax(tiles):
        p, ls = {}, []
        for b in range(pair // SUBLANES):
            mine = [tiles[b, j] for j in live[b % (GRID_W // SUBLANES)]]
            m = jnp.max(functools.reduce(jnp.maximum, mine), axis=-1, keepdims=True)
            es = [jnp.exp(t - m) for t in mine]
            for j, e in zip(live[b % (GRID_W // SUBLANES)], es):
                p[b, j] = e
            ls.append(jnp.sum(functools.reduce(jnp.add, es), axis=-1, keepdims=True))
        zero = jnp.zeros((SUBLANES, LANES), F32)
        full = jnp.concatenate(
            [jnp.concatenate([p.get((b, j), zero) for j in range(ncb)], axis=1)
             for b in range(pair // SUBLANES)], axis=0)
        return full.astype(BF16), ls

    def weighted(r, p, ls):
        rs = jnp.clip(r - NA_WH // 2, 0, rows - NA_WH)
        o = jnp.dot(p, band_rows(v_ref, rs), preferred_element_type=F32)
        nb = GRID_W // SUBLANES
        out = [jnp.where(low,
                         o[i * SUBLANES:(i + 1) * SUBLANES] / ls[i],
                         o[(nb + i) * SUBLANES:(nb + i + 1) * SUBLANES] / ls[nb + i]) for i in range(nb)]
        o_ref[tok(r), :] = jnp.concatenate(out, axis=0).astype(BF16)

    def body(g, carry):
        rs_ = [g * NA_ROW_GROUP + i for i in range(NA_ROW_GROUP)]
        ss = [scores(r) for r in rs_]
        pls = [softmax(s) for s in ss]
        for r, (p, l) in zip(rs_, pls):
            weighted(r, p, l)
        return carry

    lax.fori_loop(0, rows // NA_ROW_GROUP, body, 0)


def _na(na, layer, tab, tok0, nseq, S):
    rows = S // GRID_W
    b0 = tok0 // S
    npair = NA_HEADS // 2
    return pl.pallas_call(
        functools.partial(_na_kernel, rows=rows),
        grid=(nseq, npair),
        in_specs=[
            pl.BlockSpec((S, LANES), lambda b, hp: (b0 + b, hp)),
            pl.BlockSpec((S, LANES), lambda b, hp: (b0 + b, npair + hp)),
            pl.BlockSpec((S, LANES), lambda b, hp: (b0 + b, 2 * npair + hp)),
            pl.BlockSpec((None, 1, NA_WH, 2 * GRID_W, NA_WH * GRID_W), lambda b, hp: (layer, hp, 0, 0, 0)),
        ],
        out_specs=pl.BlockSpec((S, LANES), lambda b, hp: (b, hp)),
        out_shape=jax.ShapeDtypeStruct((nseq * S, NA_WIDTH), BF16),
        compiler_params=_cparams(("parallel", "parallel")),
        name="na",
    )(na, na, na, tab)


def _na_bias_table(rel_bias):
    H = rel_bias.shape[0]
    col = np.arange(GRID_W)
    col_start = np.clip(col - NA_WW // 2, 0, GRID_W - NA_WW)
    valid = (col[None, :] >= col_start[:, None]) & (col[None, :] < col_start[:, None] + NA_WW)
    rel = col[None, :] - col[:, None] + (NA_WW - 1)
    pick = (valid[:, :, None] & (rel[:, :, None] == np.arange(2 * NA_WW - 1))).astype(np.float32)
    cm = jnp.einsum('hij,ckj->hick', rel_bias.astype(F32), pick, precision=lax.Precision.HIGHEST)
    cm = jnp.where(valid[None, None], cm, MASK_VALUE)
    per_d = [jnp.transpose(cm[:, NA_WH - 1 - d:2 * NA_WH - 1 - d], (0, 2, 1, 3)) for d in range(NA_WH)]
    t = jnp.stack(per_d, axis=1)
    t = t.reshape(H, NA_WH, GRID_W, NA_WH, GRID_W // NA_WW, NA_WW).transpose(0, 1, 2, 4, 3, 5)
    t = t.reshape(H // 2, 2, NA_WH, GRID_W, NA_WH * GRID_W)
    return jnp.transpose(t, (0, 2, 1, 3, 4)).reshape(H // 2, NA_WH, 2 * GRID_W, NA_WH * GRID_W)


def _mix_kernel(*refs, nx, u1, n1, n2):
    x_refs, a_refs = refs[:nx], refs[nx:nx + 2]
    (cd_ref, q_ref, k_ref, v_ref, sb_ref, rg_ref, ga_ref, gr_ref, gn_ref, dm_ref, qdf_ref, qdb_ref, kdf_ref,
     wa_ref, wr_ref, wo_ref, x_out_ref, st_ref, o_ref) = refs[nx + 2:]
    blk = pl.program_id(0)
    loc, _ = _local(blk, u1, n1, n2)

    @pl.when(loc == 0)
    def _():
        st_ref[...] = jnp.zeros_like(st_ref)

    ya = jnp.dot(_read_x(a_refs, u1, blk), wa_ref[...], preferred_element_type=F32)

    for cc in range(RET_CHUNKS_PER_STEP):
        tok = slice(cc * RET_CHUNK, (cc + 1) * RET_CHUNK)
        for h in range(RET_HEADS):
            qh = q_ref[tok, h * RET_QK_DIM:(h + 1) * RET_QK_DIM]
            kh = k_ref[tok, h * RET_QK_DIM:(h + 1) * RET_QK_DIM]
            vh = v_ref[tok, h * RET_V_DIM:(h + 1) * RET_V_DIM]
            st = st_ref[h]
            s = lax.dot_general(qh.astype(BF16), kh.astype(BF16), (((1,), (1,)), ((), ())),
                                preferred_element_type=F32)
            o = jnp.dot((s * dm_ref[h]).astype(BF16), vh, preferred_element_type=F32)
            o = o + jnp.dot((qh * qdf_ref[h]).astype(BF16), st.astype(BF16), preferred_element_type=F32)
            o = o + jnp.dot((qh * qdb_ref[h]).astype(BF16), sb_ref[cc, h], preferred_element_type=F32)
            st_ref[h] = st * cd_ref[0, h] + lax.dot_general(
                (kh * kdf_ref[h]).astype(BF16), vh, (((0,), (0,)), ((), ())), preferred_element_type=F32)
            mu = jnp.mean(o, axis=-1, keepdims=True)
            oc = o - mu
            var = jnp.mean(oc * oc, axis=-1, keepdims=True)
            y = (oc * lax.rsqrt(var + EPS)) * gn_ref[:, h * RET_V_DIM:(h + 1) * RET_V_DIM]
            rg = rg_ref[tok, h * RET_V_DIM:(h + 1) * RET_V_DIM].astype(F32)
            o_ref[tok, h * RET_V_DIM:(h + 1) * RET_V_DIM] = ((rg * _sigmoid(rg)) * y).astype(BF16)

    yr = jnp.dot(o_ref[...], wr_ref[...], preferred_element_type=F32)
    mixed = _sigmoid(ga_ref[...].astype(F32)) * ya + _sigmoid(gr_ref[...].astype(F32)) * yr
    x_out_ref[...] = _read_x(x_refs, u1, blk) + jnp.dot(mixed.astype(BF16), wo_ref[...],
                                                       preferred_element_type=F32)


def _ret_tables(decay_f, decay_b):
    C = RET_CHUNK
    lgf = jax.nn.log_sigmoid(decay_f.astype(F32))
    lgb = jax.nn.log_sigmoid(decay_b.astype(F32))
    idx = jnp.arange(C, dtype=F32)
    diff = idx[:, None] - idx[None, :]
    dmat = jnp.where(diff[None] >= 0,
                     jnp.exp(jnp.maximum(diff, 0.0)[None] * lgf[:, None, None]),
                     jnp.exp(jnp.maximum(-diff, 0.0)[None] * lgb[:, None, None]))

    def rows(e):
        return jnp.broadcast_to(e[:, :, None], (RET_HEADS, C, RET_QK_DIM))

    qdf = rows(jnp.exp((idx + 1.0)[None, :] * lgf[:, None]))
    kdf = rows(jnp.exp((C - 1.0 - idx)[None, :] * lgf[:, None]))
    qdb = rows(jnp.exp((C - idx)[None, :] * lgb[:, None]))
    kdb = rows(jnp.exp(idx[None, :] * lgb[:, None]))
    cd = jnp.stack([jnp.exp(C * lgf), jnp.exp(C * lgb)])
    return dict(dmat=dmat, qdf=qdf, kdf=kdf, qdb=qdb, kdb=kdb, cd=cd)


def _mix(xs, a12, rqk, rv, sb, gts, layer, cd, gn, tabs, wa, wr, wo, u1, n1, n2):
    T = rqk.shape[0]
    C = RET_CHUNK
    tm = RET_CHUNKS_PER_STEP * C
    smem = pl.BlockSpec(memory_space=pltpu.SMEM)
    full3 = _layer_spec((RET_HEADS, C, RET_QK_DIM), layer)

    def rows(width, col=0):
        return pl.BlockSpec((tm, width), lambda i: (i, col))

    return pl.pallas_call(
        functools.partial(_mix_kernel, nx=len(xs), u1=u1, n1=n1, n2=n2),
        grid=(T // tm,),
        in_specs=_x_specs(xs, tm, u1) + _x_specs(a12, tm, u1) + [
            smem,
            rows(RET_QK_WIDTH, 0), rows(RET_QK_WIDTH, 1), rows(RET_V_WIDTH),
            pl.BlockSpec((RET_CHUNKS_PER_STEP, RET_HEADS, RET_QK_DIM, RET_V_DIM), lambda i: (i, 0, 0, 0)),
            rows(RET_V_WIDTH, 0), rows(D_MODEL, 1), rows(D_MODEL, 2),
            _layer_spec((1, RET_V_WIDTH), layer),
            _layer_spec((RET_HEADS, C, C), layer),
            full3, full3, full3,
            _layer_spec((NA_WIDTH, D_MODEL), layer),
            _layer_spec((RET_V_WIDTH, D_MODEL), layer),
            _layer_spec((D_MODEL, D_MODEL), layer),
        ],
        out_specs=rows(D_MODEL),
        out_shape=jax.ShapeDtypeStruct((T, D_MODEL), F32),
        scratch_shapes=[pltpu.VMEM((RET_HEADS, RET_QK_DIM, RET_V_DIM), F32),
                        pltpu.VMEM((tm, RET_V_WIDTH), BF16)],
        compiler_params=_cparams(("arbitrary",)),
        name="mix",
    )(*xs, *a12, cd, rqk, rqk, rv, sb, gts, gts, gts, gn,
      tabs["dmat"], tabs["qdf"], tabs["qdb"], tabs["kdf"], wa, wr, wo)


def _ffn_kernel(xp_ref, x_ref, xn_ref, g_ref, wu_ref, cw_ref, wd_ref, gf_ref, *rest, tm, u1, n1, n2, final):
    o_refs, (h_ref, a_ref, xs_ref) = rest[:-3], rest[-3:]
    sub = SUBLANES
    npl = tm // sub
    loc, n = _local(pl.program_id(0), u1, n1, n2)
    g = g_ref[...]

    xs_ref[...] = jnp.swapaxes(x_ref[...].reshape(sub, npl, D_MODEL), 0, 1).reshape(tm, D_MODEL)
    h_ref[0:tm, :] = _rms(xs_ref[...], g).astype(BF16)
    hp = jnp.where(loc == 0, 0.0, _rms(xp_ref[...], g))[sub - 1:sub]
    hn = jnp.where(loc == n - 1, 0.0, _rms(xn_ref[...], g))[0:1]
    hrow = lax.broadcasted_iota(jnp.int32, (2 * sub, D_MODEL), 0)
    h_ref[tm:tm + 2 * sub, :] = jnp.where(hrow == 0, hp, jnp.where(hrow == 1, hn, 0.0)).astype(BF16)

    def up(c0, wd):
        h = h_ref[...]
        return (jnp.dot(h, wu_ref[:, c0:c0 + wd], preferred_element_type=F32),
                jnp.dot(h, wu_ref[:, D_FF + c0:D_FF + c0 + wd], preferred_element_type=F32))

    def conv(u, c0):
        wd = u.shape[1]
        row = lax.broadcasted_iota(jnp.int32, (sub, wd), 0)
        first = jnp.where(row == 0, u[tm:tm + 1], pltpu.roll(u[tm - sub:tm], 1, axis=0))
        last = jnp.where(row == sub - 1, u[tm + 1:tm + 2], pltpu.roll(u[0:sub], sub - 1, axis=0))
        before = jnp.concatenate([first, u[0:tm - sub]], axis=0)
        after = jnp.concatenate([u[sub:tm], last], axis=0)
        return (before * cw_ref[0:1, c0:c0 + wd] + u[0:tm] * cw_ref[1:2, c0:c0 + wd]
                + after * cw_ref[2:3, c0:c0 + wd])

    tiles = [(c0, min(FFN_TN, D_FF - c0)) for c0 in range(0, D_FF, FFN_TN)]
    nxt = up(*tiles[0])
    for t, (c0, wd) in enumerate(tiles):
        ug, uv = nxt
        if t + 1 < len(tiles):
            nxt = up(*tiles[t + 1])
        a_ref[:, c0:c0 + wd] = (_gelu_tanh(conv(ug, c0)) * conv(uv, D_FF + c0)).astype(BF16)

    y = jnp.dot(a_ref[...], wd_ref[...], preferred_element_type=F32)
    y = xs_ref[...] + y
    if final:
        y = _rms(y, gf_ref[...])
    y = jnp.swapaxes(y.reshape(npl, sub, D_MODEL), 0, 1).reshape(tm, D_MODEL)
    if len(o_refs) == 1:
        o_refs[0][...] = y
    else:
        @pl.when(pl.program_id(0) < u1)
        def _():
            o_refs[0][...] = y

        @pl.when(pl.program_id(0) >= u1)
        def _():
            o_refs[1][...] = y


def _ffn(x, layer, g, w_up, conv_w, w_down, gf, tm, u1, n1, n2, final):
    T = x.shape[0]
    hb = tm // SUBLANES
    last = T // SUBLANES - 1
    resident = dict(pipeline_mode=pl.Buffered(1))
    if final:
        out_specs = [pl.BlockSpec((tm, D_MODEL), lambda i: (jnp.minimum(i, u1 - 1), 0)),
                     pl.BlockSpec((tm, D_MODEL), lambda i: (jnp.maximum(i - u1, 0), 0))]
        out_shape = [jax.ShapeDtypeStruct((u1 * tm, D_MODEL), F32),
                     jax.ShapeDtypeStruct((T - u1 * tm, D_MODEL), F32)]
    else:
        out_specs = pl.BlockSpec((tm, D_MODEL), lambda i: (i, 0))
        out_shape = jax.ShapeDtypeStruct((T, D_MODEL), F32)
    return pl.pallas_call(
        functools.partial(_ffn_kernel, tm=tm, u1=u1, n1=n1, n2=n2, final=final),
        grid=(T // tm,),
        in_specs=[
            pl.BlockSpec((SUBLANES, D_MODEL), lambda i: (jnp.maximum(i * hb - 1, 0), 0)),
            pl.BlockSpec((tm, D_MODEL), lambda i: (i, 0)),
            pl.BlockSpec((SUBLANES, D_MODEL), lambda i: (jnp.minimum((i + 1) * hb, last), 0)),
            _layer_spec((1, D_MODEL), layer),
            _layer_spec((D_MODEL, 2 * D_FF), layer, **resident),
            _layer_spec((3, 2 * D_FF), layer, **resident),
            _layer_spec((D_FF, D_MODEL), layer, **resident),
            pl.BlockSpec((1, D_MODEL), lambda i: (0, 0)),
        ],
        out_specs=out_specs,
        out_shape=out_shape,
        scratch_shapes=[
            pltpu.VMEM((tm + 2 * SUBLANES, D_MODEL), BF16),
            pltpu.VMEM((tm, D_FF), BF16),
            pltpu.VMEM((tm, D_MODEL), F32),
        ],
        compiler_params=_cparams(("arbitrary",)),
        name="ffn",
    )(x, x, x, g, w_up, conv_w, w_down, gf)


def _rope_tables(seq_len):
    inv_freq = ROPE_BASE ** (-jnp.arange(0, RET_QK_DIM, 2, dtype=F32) / RET_QK_DIM)
    ang = jnp.arange(seq_len, dtype=F32)[:, None] * inv_freq[None, :]
    cos, sin = jnp.cos(ang), jnp.sin(ang)
    return jnp.concatenate([cos, cos], axis=1), jnp.concatenate([-sin, sin], axis=1)


def _pick_tile(limit, *sizes):
    t = limit
    while any(s % t for s in sizes):
        t //= 2
    return t


def kernel(x_prompt, x_sample, norm_mix_g, w_in, na_rel_bias, ret_decay_fwd, ret_decay_bwd, ret_norm_g,
           w_branch_attn, w_branch_ret, w_out, norm_ffn_g, w_up, ffn_conv_w, w_down, norm_final_g):
    B1, S1, D = x_prompt.shape
    B2, S2, _ = x_sample.shape
    T1, T2 = B1 * S1, B2 * S2
    T = T1 + T2
    depth = w_in.shape[0]
    assert D == D_MODEL and w_in.shape[2] == D_IN
    assert S1 % (NA_WH * GRID_W) == 0 and S2 % (NA_WH * GRID_W) == 0 and T1 % S2 == 0

    xs = (x_prompt.reshape(T1, D), x_sample.reshape(T2, D))
    cos2, sin2 = _rope_tables(max(S1, S2))

    tm_proj = _pick_tile(512, S1, S2)
    tm_mix = _pick_tile(512, S1, S2)

    def units(t):
        return T1 // t, S1 // t, S2 // t

    def gain(g):
        return g.astype(F32)[:, None, :]

    w_in_b, w_up_b, w_down_b = w_in.astype(BF16), w_up.astype(BF16), w_down.astype(BF16)
    wa_b, wr_b, wo_b = w_branch_attn.astype(BF16), w_branch_ret.astype(BF16), w_out.astype(BF16)
    g_mix, g_ffn, g_ret, conv_w = gain(norm_mix_g), gain(norm_ffn_g), gain(ret_norm_g), ffn_conv_w.astype(F32)
    tabs = jax.vmap(_ret_tables)(ret_decay_fwd, ret_decay_bwd)
    heads = na_rel_bias.shape[1]
    tab = _na_bias_table(na_rel_bias.reshape((depth * heads,) + na_rel_bias.shape[2:]))
    tab = tab.reshape((depth, heads // 2) + tab.shape[1:])

    for l in range(depth):
        na, rqk, rv, gts, sb = _inproj(xs, l, g_mix, w_in_b, cos2, sin2, tabs["cd"][l], tabs["kdb"],
                                       tm_proj, *units(tm_proj))
        a = (_na(na, l, tab, 0, B1, S1), _na(na, l, tab, T1, B2, S2))
        x = _mix(xs, a, rqk, rv, sb, gts, l, tabs["cd"][l], g_ret, tabs, wa_b, wr_b, wo_b,
                 *units(RET_CHUNK * RET_CHUNKS_PER_STEP))
        x = _ffn(x, l, g_ffn, w_up_b, conv_w, w_down_b, norm_final_g[None].astype(F32), tm_mix, *units(tm_mix),
                 final=(l == depth - 1))
        xs = (x,)

    y1, y2 = x
    return y1.reshape(B1, S1, D), y2.reshape(B2, S2, D)
```

```python
import functools
import math

import numpy as np
import jax
import jax.numpy as jnp
from jax import lax
from jax.experimental import pallas as pl
from jax.experimental.pallas import tpu as pltpu

F32 = jnp.float32
BF16 = jnp.bfloat16

D_MODEL = 1024
GRID_W = 64
NA_HEADS = 8
NA_HEAD_DIM = 64
NA_WIDTH = NA_HEADS * NA_HEAD_DIM
NA_WH = 8
NA_WW = 16
RET_HEADS = 4
RET_QK_DIM = 128
RET_V_DIM = 256
RET_QK_WIDTH = RET_HEADS * RET_QK_DIM
RET_V_WIDTH = RET_HEADS * RET_V_DIM
RET_CHUNK = 256
ROPE_BASE = 10000.0
D_FF = 2816
EPS = 1e-6
D_IN = 3 * NA_WIDTH + 2 * RET_QK_WIDTH + 2 * RET_V_WIDTH + 2 * D_MODEL

LANES = 128
SUBLANES = 8
PROJ_TN = 512
FFN_TN = 256
RET_CHUNKS_PER_STEP = 2
NA_ROW_GROUP = 32
MASK_VALUE = -1e30
VMEM_LIMIT = 56 * 1024 * 1024


def _cparams(sem):
    return pltpu.CompilerParams(dimension_semantics=sem, vmem_limit_bytes=VMEM_LIMIT)


def _layer_spec(shape, layer, **kwargs):
    return pl.BlockSpec((None,) + tuple(shape), lambda *_: (layer,) + (0,) * len(shape), **kwargs)


def _local(u, u1, n1, n2):
    in1 = u < u1
    loc = jnp.where(in1, lax.rem(u, n1), lax.rem(jnp.maximum(u - u1, 0), n2))
    return loc, jnp.where(in1, n1, n2)


def _rms(x, g):
    ms = jnp.mean(x * x, axis=-1, keepdims=True)
    return (x * lax.rsqrt(ms + EPS)) * g


def _sigmoid(x):
    return 0.5 * (jnp.tanh(0.5 * x) + 1.0)


def _gelu_tanh(x):
    c = np.float32(np.sqrt(2.0 / np.pi))
    return x * (0.5 * (1.0 + jnp.tanh(c * (x + np.float32(0.044715) * (x * x * x)))))


def _x_specs(xs, tm, u1, nblk=None):
    width = xs[0].shape[1]

    def blk(i):
        return i if nblk is None else nblk - 1 - i

    if len(xs) == 1:
        return [pl.BlockSpec((tm, width), lambda i: (blk(i), 0))]
    return [pl.BlockSpec((tm, width), lambda i: (jnp.minimum(blk(i), u1 - 1), 0)),
            pl.BlockSpec((tm, width), lambda i: (jnp.maximum(blk(i) - u1, 0), 0))]


def _read_x(x_refs, u1, blk):
    if len(x_refs) == 1:
        return x_refs[0][...]
    return jnp.where(blk < u1, x_refs[0][...], x_refs[1][...])


def _inproj_kernel(*refs, nx, u1, n1, n2):
    x_refs = refs[:nx]
    (g_ref, w_ref, cos_ref, sin_ref, cd_ref, kdb_ref,
     na_ref, rqk_ref, rv_ref, gts_ref, sb_ref, h_ref, st_ref) = refs[nx:]
    blk = pl.num_programs(0) - 1 - pl.program_id(0)
    loc, n = _local(blk, u1, n1, n2)
    h_ref[...] = _rms(_read_x(x_refs, u1, blk), g_ref[...]).astype(BF16)

    @pl.when(loc == n - 1)
    def _():
        st_ref[...] = jnp.zeros_like(st_ref)

    def proj(c0):
        return jnp.dot(h_ref[...], w_ref[:, c0:c0 + PROJ_TN], preferred_element_type=F32)

    def rope(acc, c0, scale):
        cos2 = cos_ref[...]
        sin2 = sin_ref[...]
        for hd in range(PROJ_TN // RET_QK_DIM):
            xh = acc[:, hd * RET_QK_DIM:(hd + 1) * RET_QK_DIM]
            r = xh * cos2 + pltpu.roll(xh, RET_QK_DIM // 2, axis=1) * sin2
            if scale is not None:
                r = r * scale
            rqk_ref[:, c0 + hd * RET_QK_DIM:c0 + (hd + 1) * RET_QK_DIM] = r

    def sweep_chunk(cc):
        tok = slice(cc * RET_CHUNK, (cc + 1) * RET_CHUNK)
        for h in range(RET_HEADS):
            st = st_ref[h]
            sb_ref[cc, h] = st.astype(BF16)
            k = rqk_ref[tok, RET_QK_WIDTH + h * RET_QK_DIM:RET_QK_WIDTH + (h + 1) * RET_QK_DIM]
            vh = rv_ref[tok, h * RET_V_DIM:(h + 1) * RET_V_DIM]
            st_ref[h] = st * cd_ref[1, h] + lax.dot_general(
                (k * kdb_ref[h]).astype(BF16), vh, (((0,), (0,)), ((), ())), preferred_element_type=F32)

    c_na, c_rq = 0, 3 * NA_WIDTH
    c_rk, c_rv = c_rq + RET_QK_WIDTH, c_rq + 2 * RET_QK_WIDTH
    c_gts = c_rv + RET_V_WIDTH
    rope(proj(c_rk), RET_QK_WIDTH, np.float32(RET_QK_DIM ** -0.5))
    for c0 in range(0, RET_V_WIDTH, PROJ_TN):
        rv_ref[:, c0:c0 + PROJ_TN] = proj(c_rv + c0).astype(BF16)
    pending = list(reversed(range(RET_CHUNKS_PER_STEP)))
    for c0 in range(0, 3 * NA_WIDTH, PROJ_TN):
        na_ref[:, c0:c0 + PROJ_TN] = proj(c_na + c0).astype(BF16)
        if pending:
            sweep_chunk(pending.pop(0))
    rope(proj(c_rq), 0, None)
    for c0 in range(0, RET_V_WIDTH + 2 * D_MODEL, PROJ_TN):
        if pending:
            sweep_chunk(pending.pop(0))
        gts_ref[:, c0:c0 + PROJ_TN] = proj(c_gts + c0).astype(BF16)


def _inproj(xs, layer, g, w, cos2, sin2, cd, kdb, tm, u1, n1, n2):
    T = sum(x.shape[0] for x in xs)
    nblk = T // tm
    cps = RET_CHUNKS_PER_STEP
    assert RET_QK_WIDTH == PROJ_TN and tm == cps * RET_CHUNK

    def tab_map(i):
        return (_local(nblk - 1 - i, u1, n1, n2)[0], 0)

    def rows(width):
        return pl.BlockSpec((tm, width), lambda i: (nblk - 1 - i, 0))

    return pl.pallas_call(
        functools.partial(_inproj_kernel, nx=len(xs), u1=u1, n1=n1, n2=n2),
        grid=(nblk,),
        in_specs=_x_specs(xs, tm, u1, nblk) + [
            _layer_spec((1, D_MODEL), layer),
            _layer_spec((D_MODEL, D_IN), layer, pipeline_mode=pl.Buffered(1)),
            pl.BlockSpec((tm, RET_QK_DIM), tab_map),
            pl.BlockSpec((tm, RET_QK_DIM), tab_map),
            pl.BlockSpec(memory_space=pltpu.SMEM),
            _layer_spec((RET_HEADS, RET_CHUNK, RET_QK_DIM), layer),
        ],
        out_specs=[
            rows(3 * NA_WIDTH), rows(2 * RET_QK_WIDTH), rows(RET_V_WIDTH), rows(RET_V_WIDTH + 2 * D_MODEL),
            pl.BlockSpec((cps, RET_HEADS, RET_QK_DIM, RET_V_DIM), lambda i: (nblk - 1 - i, 0, 0, 0)),
        ],
        out_shape=[
            jax.ShapeDtypeStruct((T, 3 * NA_WIDTH), BF16),
            jax.ShapeDtypeStruct((T, 2 * RET_QK_WIDTH), F32),
            jax.ShapeDtypeStruct((T, RET_V_WIDTH), BF16),
            jax.ShapeDtypeStruct((T, RET_V_WIDTH + 2 * D_MODEL), BF16),
            jax.ShapeDtypeStruct((nblk * cps, RET_HEADS, RET_QK_DIM, RET_V_DIM), BF16),
        ],
        scratch_shapes=[pltpu.VMEM((tm, D_MODEL), BF16),
                        pltpu.VMEM((RET_HEADS, RET_QK_DIM, RET_V_DIM), F32)],
        compiler_params=_cparams(("arbitrary",)),
        name="inproj",
    )(*xs, g, w, cos2, sin2, cd, kdb)


def _na_kernel(q_ref, k_ref, v_ref, tab_ref, o_ref, *, rows):
    pair = 2 * GRID_W
    lane = lax.broadcasted_iota(jnp.int32, (pair, LANES), 1)
    sub = lax.broadcasted_iota(jnp.int32, (pair, LANES), 0)
    keep = (lane < NA_HEAD_DIM) == (sub < GRID_W)
    low = lax.broadcasted_iota(jnp.int32, (SUBLANES, LANES), 1) < NA_HEAD_DIM
    scale = NA_HEAD_DIM ** -0.5

    def tok(row):
        return pl.ds(pl.multiple_of(row * GRID_W, GRID_W), GRID_W)

    ncb = GRID_W // NA_WW
    qcol = np.arange(GRID_W)
    qstart = np.clip(qcol - NA_WW // 2, 0, GRID_W - NA_WW)
    live = []
    for i in range(GRID_W // SUBLANES):
        lo, hi = qstart[i * SUBLANES], qstart[(i + 1) * SUBLANES - 1] + NA_WW
        live.append([j for j in range(ncb) if j * NA_WW < hi and (j + 1) * NA_WW > lo])

    def band_rows(ref, rs):
        rows_ = ref[pl.ds(pl.multiple_of(rs * GRID_W, GRID_W), NA_WH * GRID_W), :]
        return jnp.concatenate(
            [rows_[w * GRID_W + j * NA_WW:w * GRID_W + (j + 1) * NA_WW]
             for j in range(ncb) for w in range(NA_WH)], axis=0)

    def scores(r):
        rs = jnp.clip(r - NA_WH // 2, 0, rows - NA_WH)
        q = q_ref[tok(r), :] * scale
        qm = jnp.where(keep, jnp.concatenate([q, q], axis=0), jnp.zeros((pair, LANES), BF16))
        s = lax.dot_general(qm, band_rows(k_ref, rs), (((1,), (1,)), ((), ())), preferred_element_type=F32)
        d = r - rs
        tiles = {}
        for b in range(pair // SUBLANES):
            for j in live[b % (GRID_W // SUBLANES)]:
                rows8, lanes = slice(b * SUBLANES, (b + 1) * SUBLANES), slice(j * LANES, (j + 1) * LANES)
                tiles[b, j] = s[rows8, lanes] + tab_ref[0, d, rows8, lanes]
        return tiles

    def softmax(tiles):
        p, ls = {}, []
        for b in range(pair // SUBLANES):
            mine = [tiles[b, j] for j in live[b % (GRID_W // SUBLANES)]]
            m = jnp.max(functools.reduce(jnp.maximum, mine), axis=-1, keepdims=True)
            es = [jnp.exp(t - m) for t in mine]
            for j, e in zip(live[b % (GRID_W // SUBLANES)], es):
                p[b, j] = e
            ls.append(jnp.sum(functools.reduce(jnp.add, es), axis=-1, keepdims=True))
        zero = jnp.zeros((SUBLANES, LANES), F32)
        full = jnp.concatenate(
            [jnp.concatenate([p.get((b, j), zero) for j in range(ncb)], axis=1)
             for b in range(pair // SUBLANES)], axis=0)
        return full.astype(BF16), ls

    def weighted(r, p, ls):
        rs = jnp.clip(r - NA_WH // 2, 0, rows - NA_WH)
        o = jnp.dot(p, band_rows(v_ref, rs), preferred_element_type=F32)
        nb = GRID_W // SUBLANES
        out = [jnp.where(low,
                         o[i * SUBLANES:(i + 1) * SUBLANES] / ls[i],
                         o[(nb + i) * SUBLANES:(nb + i + 1) * SUBLANES] / ls[nb + i]) for i in range(nb)]
        o_ref[tok(r), :] = jnp.concatenate(out, axis=0).astype(BF16)

    group = math.gcd(rows, NA_ROW_GROUP)

    def body(g, carry):
        rs_ = [g * group + i for i in range(group)]
        ss = [scores(r) for r in rs_]
        pls = [softmax(s) for s in ss]
        for r, (p, l) in zip(rs_, pls):
            weighted(r, p, l)
        return carry

    lax.fori_loop(0, rows // group, body, 0)


def _na(na, layer, tab, tok0, nseq, S):
    rows = S // GRID_W
    b0 = tok0 // S
    npair = NA_HEADS // 2
    return pl.pallas_call(
        functools.partial(_na_kernel, rows=rows),
        grid=(nseq, npair),
        in_specs=[
            pl.BlockSpec((S, LANES), lambda b, hp: (b0 + b, hp)),
            pl.BlockSpec((S, LANES), lambda b, hp: (b0 + b, npair + hp)),
            pl.BlockSpec((S, LANES), lambda b, hp: (b0 + b, 2 * npair + hp)),
            pl.BlockSpec((None, 1, NA_WH, 2 * GRID_W, NA_WH * GRID_W), lambda b, hp: (layer, hp, 0, 0, 0)),
        ],
        out_specs=pl.BlockSpec((S, LANES), lambda b, hp: (b, hp)),
        out_shape=jax.ShapeDtypeStruct((nseq * S, NA_WIDTH), BF16),
        compiler_params=_cparams(("parallel", "parallel")),
        name="na",
    )(na, na, na, tab)


def _na_bias_table(rel_bias):
    H = rel_bias.shape[0]
    col = np.arange(GRID_W)
    col_start = np.clip(col - NA_WW // 2, 0, GRID_W - NA_WW)
    valid = (col[None, :] >= col_start[:, None]) & (col[None, :] < col_start[:, None] + NA_WW)
    rel = col[None, :] - col[:, None] + (NA_WW - 1)
    pick = (valid[:, :, None] & (rel[:, :, None] == np.arange(2 * NA_WW - 1))).astype(np.float32)
    cm = jnp.einsum('hij,ckj->hick', rel_bias.astype(F32), pick, precision=lax.Precision.HIGHEST)
    cm = jnp.where(valid[None, None], cm, MASK_VALUE)
    per_d = [jnp.transpose(cm[:, NA_WH - 1 - d:2 * NA_WH - 1 - d], (0, 2, 1, 3)) for d in range(NA_WH)]
    t = jnp.stack(per_d, axis=1)
    t = t.reshape(H, NA_WH, GRID_W, NA_WH, GRID_W // NA_WW, NA_WW).transpose(0, 1, 2, 4, 3, 5)
    t = t.reshape(H // 2, 2, NA_WH, GRID_W, NA_WH * GRID_W)
    return jnp.transpose(t, (0, 2, 1, 3, 4)).reshape(H // 2, NA_WH, 2 * GRID_W, NA_WH * GRID_W)


def _mix_kernel(*refs, nx, u1, n1, n2):
    x_refs, a_refs = refs[:nx], refs[nx:nx + 2]
    (cd_ref, qk_ref, v_ref, sb_ref, gts_ref, gn_ref, dm_ref, qdf_ref, qdb_ref, kdf_ref,
     wa_ref, wr_ref, wo_ref, x_out_ref, st_ref, o_ref) = refs[nx + 2:]
    g_attn, g_ret = RET_V_WIDTH, RET_V_WIDTH + D_MODEL
    blk = pl.program_id(0)
    loc, _ = _local(blk, u1, n1, n2)

    @pl.when(loc == 0)
    def _():
        st_ref[...] = jnp.zeros_like(st_ref)

    ya = jnp.dot(_read_x(a_refs, u1, blk), wa_ref[...], preferred_element_type=F32)

    for cc in range(RET_CHUNKS_PER_STEP):
        tok = slice(cc * RET_CHUNK, (cc + 1) * RET_CHUNK)
        for h in range(RET_HEADS):
            qh = qk_ref[tok, h * RET_QK_DIM:(h + 1) * RET_QK_DIM]
            kh = qk_ref[tok, RET_QK_WIDTH + h * RET_QK_DIM:RET_QK_WIDTH + (h + 1) * RET_QK_DIM]
            vh = v_ref[tok, h * RET_V_DIM:(h + 1) * RET_V_DIM]
            st = st_ref[h]
            s = lax.dot_general(qh.astype(BF16), kh.astype(BF16), (((1,), (1,)), ((), ())),
                                preferred_element_type=F32)
            o = jnp.dot((s * dm_ref[h]).astype(BF16), vh, preferred_element_type=F32)
            o = o + jnp.dot((qh * qdf_ref[h]).astype(BF16), st.astype(BF16), preferred_element_type=F32)
            o = o + jnp.dot((qh * qdb_ref[h]).astype(BF16), sb_ref[cc, h], preferred_element_type=F32)
            st_ref[h] = st * cd_ref[0, h] + lax.dot_general(
                (kh * kdf_ref[h]).astype(BF16), vh, (((0,), (0,)), ((), ())), preferred_element_type=F32)
            mu = jnp.mean(o, axis=-1, keepdims=True)
            oc = o - mu
            var = jnp.mean(oc * oc, axis=-1, keepdims=True)
            y = (oc * lax.rsqrt(var + EPS)) * gn_ref[:, h * RET_V_DIM:(h + 1) * RET_V_DIM]
            rg = gts_ref[tok, h * RET_V_DIM:(h + 1) * RET_V_DIM].astype(F32)
            o_ref[tok, h * RET_V_DIM:(h + 1) * RET_V_DIM] = ((rg * _sigmoid(rg)) * y).astype(BF16)

    yr = jnp.dot(o_ref[...], wr_ref[...], preferred_element_type=F32)
    mixed = (_sigmoid(gts_ref[:, g_attn:g_attn + D_MODEL].astype(F32)) * ya
             + _sigmoid(gts_ref[:, g_ret:g_ret + D_MODEL].astype(F32)) * yr)
    x_out_ref[...] = _read_x(x_refs, u1, blk) + jnp.dot(mixed.astype(BF16), wo_ref[...],
                                                       preferred_element_type=F32)


def _ret_tables(decay_f, decay_b):
    C = RET_CHUNK
    lgf = jax.nn.log_sigmoid(decay_f.astype(F32))
    lgb = jax.nn.log_sigmoid(decay_b.astype(F32))
    idx = jnp.arange(C, dtype=F32)
    diff = idx[:, None] - idx[None, :]
    dmat = jnp.where(diff[None] >= 0,
                     jnp.exp(jnp.maximum(diff, 0.0)[None] * lgf[:, None, None]),
                     jnp.exp(jnp.maximum(-diff, 0.0)[None] * lgb[:, None, None]))

    def rows(e):
        return jnp.broadcast_to(e[:, :, None], (RET_HEADS, C, RET_QK_DIM))

    qdf = rows(jnp.exp((idx + 1.0)[None, :] * lgf[:, None]))
    kdf = rows(jnp.exp((C - 1.0 - idx)[None, :] * lgf[:, None]))
    qdb = rows(jnp.exp((C - idx)[None, :] * lgb[:, None]))
    kdb = rows(jnp.exp(idx[None, :] * lgb[:, None]))
    cd = jnp.stack([jnp.exp(C * lgf), jnp.exp(C * lgb)])
    return dict(dmat=dmat, qdf=qdf, kdf=kdf, qdb=qdb, kdb=kdb, cd=cd)


def _mix(xs, a12, rqk, rv, sb, gts, layer, cd, gn, tabs, wa, wr, wo, u1, n1, n2):
    T = rqk.shape[0]
    C = RET_CHUNK
    tm = RET_CHUNKS_PER_STEP * C
    smem = pl.BlockSpec(memory_space=pltpu.SMEM)
    full3 = _layer_spec((RET_HEADS, C, RET_QK_DIM), layer)

    def rows(width, col=0):
        return pl.BlockSpec((tm, width), lambda i: (i, col))

    return pl.pallas_call(
        functools.partial(_mix_kernel, nx=len(xs), u1=u1, n1=n1, n2=n2),
        grid=(T // tm,),
        in_specs=_x_specs(xs, tm, u1) + _x_specs(a12, tm, u1) + [
            smem,
            rows(2 * RET_QK_WIDTH), rows(RET_V_WIDTH),
            pl.BlockSpec((RET_CHUNKS_PER_STEP, RET_HEADS, RET_QK_DIM, RET_V_DIM), lambda i: (i, 0, 0, 0)),
            rows(RET_V_WIDTH + 2 * D_MODEL),
            _layer_spec((1, RET_V_WIDTH), layer),
            _layer_spec((RET_HEADS, C, C), layer),
            full3, full3, full3,
            _layer_spec((NA_WIDTH, D_MODEL), layer),
            _layer_spec((RET_V_WIDTH, D_MODEL), layer),
            _layer_spec((D_MODEL, D_MODEL), layer),
        ],
        out_specs=rows(D_MODEL),
        out_shape=jax.ShapeDtypeStruct((T, D_MODEL), F32),
        scratch_shapes=[pltpu.VMEM((RET_HEADS, RET_QK_DIM, RET_V_DIM), F32),
                        pltpu.VMEM((tm, RET_V_WIDTH), BF16)],
        compiler_params=_cparams(("arbitrary",)),
        name="mix",
    )(*xs, *a12, cd, rqk, rv, sb, gts, gn,
      tabs["dmat"], tabs["qdf"], tabs["qdb"], tabs["kdf"], wa, wr, wo)


def _ffn_kernel(xp_ref, x_ref, xn_ref, g_ref, wu_ref, cw_ref, wd_ref, gf_ref, *rest, tm, u1, n1, n2, final):
    o_refs, (h_ref, a_ref, xs_ref) = rest[:-3], rest[-3:]
    sub = SUBLANES
    npl = tm // sub
    loc, n = _local(pl.program_id(0), u1, n1, n2)
    g = g_ref[...]

    xs_ref[...] = jnp.swapaxes(x_ref[...].reshape(sub, npl, D_MODEL), 0, 1).reshape(tm, D_MODEL)
    h_ref[0:tm, :] = _rms(xs_ref[...], g).astype(BF16)
    hp = jnp.where(loc == 0, 0.0, _rms(xp_ref[...], g))[sub - 1:sub]
    hn = jnp.where(loc == n - 1, 0.0, _rms(xn_ref[...], g))[0:1]
    hrow = lax.broadcasted_iota(jnp.int32, (2 * sub, D_MODEL), 0)
    h_ref[tm:tm + 2 * sub, :] = jnp.where(hrow == 0, hp, jnp.where(hrow == 1, hn, 0.0)).astype(BF16)

    def up(c0, wd):
        h = h_ref[...]
        return (jnp.dot(h, wu_ref[:, c0:c0 + wd], preferred_element_type=F32),
                jnp.dot(h, wu_ref[:, D_FF + c0:D_FF + c0 + wd], preferred_element_type=F32))

    def conv(u, c0):
        wd = u.shape[1]
        row = lax.broadcasted_iota(jnp.int32, (sub, wd), 0)
        first = jnp.where(row == 0, u[tm:tm + 1], pltpu.roll(u[tm - sub:tm], 1, axis=0))
        last = jnp.where(row == sub - 1, u[tm + 1:tm + 2], pltpu.roll(u[0:sub], sub - 1, axis=0))
        before = jnp.concatenate([first, u[0:tm - sub]], axis=0)
        after = jnp.concatenate([u[sub:tm], last], axis=0)
        return (before * cw_ref[0:1, c0:c0 + wd] + u[0:tm] * cw_ref[1:2, c0:c0 + wd]
                + after * cw_ref[2:3, c0:c0 + wd])

    tiles = [(c0, min(FFN_TN, D_FF - c0)) for c0 in range(0, D_FF, FFN_TN)]
    nxt = up(*tiles[0])
    for t, (c0, wd) in enumerate(tiles):
        ug, uv = nxt
        if t + 1 < len(tiles):
            nxt = up(*tiles[t + 1])
        a_ref[:, c0:c0 + wd] = (_gelu_tanh(conv(ug, c0)) * conv(uv, D_FF + c0)).astype(BF16)

    y = jnp.dot(a_ref[...], wd_ref[...], preferred_element_type=F32)
    y = xs_ref[...] + y
    if final:
        y = _rms(y, gf_ref[...])
    y = jnp.swapaxes(y.reshape(npl, sub, D_MODEL), 0, 1).reshape(tm, D_MODEL)
    if len(o_refs) == 1:
        o_refs[0][...] = y
    else:
        @pl.when(pl.program_id(0) < u1)
        def _():
            o_refs[0][...] = y

        @pl.when(pl.program_id(0) >= u1)
        def _():
            o_refs[1][...] = y


def _ffn(x, layer, g, w_up, conv_w, w_down, gf, tm, u1, n1, n2, final):
    T = x.shape[0]
    hb = tm // SUBLANES
    last = T // SUBLANES - 1
    resident = dict(pipeline_mode=pl.Buffered(1))
    if final:
        out_specs = [pl.BlockSpec((tm, D_MODEL), lambda i: (jnp.minimum(i, u1 - 1), 0)),
                     pl.BlockSpec((tm, D_MODEL), lambda i: (jnp.maximum(i - u1, 0), 0))]
        out_shape = [jax.ShapeDtypeStruct((u1 * tm, D_MODEL), F32),
                     jax.ShapeDtypeStruct((T - u1 * tm, D_MODEL), F32)]
    else:
        out_specs = pl.BlockSpec((tm, D_MODEL), lambda i: (i, 0))
        out_shape = jax.ShapeDtypeStruct((T, D_MODEL), F32)
    return pl.pallas_call(
        functools.partial(_ffn_kernel, tm=tm, u1=u1, n1=n1, n2=n2, final=final),
        grid=(T // tm,),
        in_specs=[
            pl.BlockSpec((SUBLANES, D_MODEL), lambda i: (jnp.maximum(i * hb - 1, 0), 0)),
            pl.BlockSpec((tm, D_MODEL), lambda i: (i, 0)),
            pl.BlockSpec((SUBLANES, D_MODEL), lambda i: (jnp.minimum((i + 1) * hb, last), 0)),
            _layer_spec((1, D_MODEL), layer),
            _layer_spec((D_MODEL, 2 * D_FF), layer, **resident),
            _layer_spec((3, 2 * D_FF), layer, **resident),
            _layer_spec((D_FF, D_MODEL), layer, **resident),
            pl.BlockSpec((1, D_MODEL), lambda i: (0, 0)),
        ],
        out_specs=out_specs,
        out_shape=out_shape,
        scratch_shapes=[
            pltpu.VMEM((tm + 2 * SUBLANES, D_MODEL), BF16),
            pltpu.VMEM((tm, D_FF), BF16),
            pltpu.VMEM((tm, D_MODEL), F32),
        ],
        compiler_params=_cparams(("arbitrary",)),
        name="ffn",
    )(x, x, x, g, w_up, conv_w, w_down, gf)


def _rope_tables(seq_len):
    inv_freq = ROPE_BASE ** (-jnp.arange(0, RET_QK_DIM, 2, dtype=F32) / RET_QK_DIM)
    ang = jnp.arange(seq_len, dtype=F32)[:, None] * inv_freq[None, :]
    cos, sin = jnp.cos(ang), jnp.sin(ang)
    return jnp.concatenate([cos, cos], axis=1), jnp.concatenate([-sin, sin], axis=1)


def _pick_tile(limit, *sizes):
    t = limit
    while any(s % t for s in sizes):
        t //= 2
    return t


def kernel(x_prompt, x_sample, norm_mix_g, w_in, na_rel_bias, ret_decay_fwd, ret_decay_bwd, ret_norm_g,
           w_branch_attn, w_branch_ret, w_out, norm_ffn_g, w_up, ffn_conv_w, w_down, norm_final_g):
    B1, S1, D = x_prompt.shape
    B2, S2, _ = x_sample.shape
    T1, T2 = B1 * S1, B2 * S2
    T = T1 + T2
    depth = w_in.shape[0]
    assert D == D_MODEL and w_in.shape[2] == D_IN
    assert S1 % (NA_WH * GRID_W) == 0 and S2 % (NA_WH * GRID_W) == 0 and T1 % S2 == 0

    xs = (x_prompt.reshape(T1, D), x_sample.reshape(T2, D))
    cos2, sin2 = _rope_tables(max(S1, S2))

    tm_proj = _pick_tile(512, S1, S2)
    tm_mix = _pick_tile(512, S1, S2)

    def units(t):
        return T1 // t, S1 // t, S2 // t

    def gain(g):
        return g.astype(F32)[:, None, :]

    w_in_b, w_up_b, w_down_b = w_in.astype(BF16), w_up.astype(BF16), w_down.astype(BF16)
    wa_b, wr_b, wo_b = w_branch_attn.astype(BF16), w_branch_ret.astype(BF16), w_out.astype(BF16)
    g_mix, g_ffn, g_ret, conv_w = gain(norm_mix_g), gain(norm_ffn_g), gain(ret_norm_g), ffn_conv_w.astype(F32)
    tabs = jax.vmap(_ret_tables)(ret_decay_fwd, ret_decay_bwd)
    heads = na_rel_bias.shape[1]
    tab = _na_bias_table(na_rel_bias.reshape((depth * heads,) + na_rel_bias.shape[2:]))
    tab = tab.reshape((depth, heads // 2) + tab.shape[1:])

    for l in range(depth):
        na, rqk, rv, gts, sb = _inproj(xs, l, g_mix, w_in_b, cos2, sin2, tabs["cd"][l], tabs["kdb"],
                                       tm_proj, *units(tm_proj))
        a = (_na(na, l, tab, 0, B1, S1), _na(na, l, tab, T1, B2, S2))
        x = _mix(xs, a, rqk, rv, sb, gts, l, tabs["cd"][l], g_ret, tabs, wa_b, wr_b, wo_b,
                 *units(RET_CHUNK * RET_CHUNKS_PER_STEP))
        x = _ffn(x, l, g_ffn, w_up_b, conv_w, w_down_b, norm_final_g[None].astype(F32), tm_mix, *units(tm_mix),
                 final=(l == depth - 1))
        xs = (x,)

    y1, y2 = x
    return y1.reshape(B1, S1, D), y2.reshape(B2, S2, D)
```

```python
import functools
import math

import numpy as np
import jax
import jax.numpy as jnp
from jax import lax
from jax.experimental import pallas as pl
from jax.experimental.pallas import tpu as pltpu

F32 = jnp.float32
BF16 = jnp.bfloat16

D_MODEL = 1024
GRID_W = 64
NA_HEADS = 8
NA_HEAD_DIM = 64
NA_WIDTH = NA_HEADS * NA_HEAD_DIM
NA_WH = 8
NA_WW = 16
RET_HEADS = 4
RET_QK_DIM = 128
RET_V_DIM = 256
RET_QK_WIDTH = RET_HEADS * RET_QK_DIM
RET_V_WIDTH = RET_HEADS * RET_V_DIM
RET_CHUNK = 256
ROPE_BASE = 10000.0
D_FF = 2816
EPS = 1e-6
D_IN = 3 * NA_WIDTH + 2 * RET_QK_WIDTH + 2 * RET_V_WIDTH + 2 * D_MODEL

LANES = 128
SUBLANES = 8
PROJ_TN = 512
FFN_TN = 256
RET_CHUNKS_PER_STEP = 2
NA_ROW_GROUP = 64
MASK_VALUE = -1e30
VMEM_LIMIT = 56 * 1024 * 1024


def _cparams(sem):
    return pltpu.CompilerParams(dimension_semantics=sem, vmem_limit_bytes=VMEM_LIMIT)


def _layer_spec(shape, layer, **kwargs):
    return pl.BlockSpec((None,) + tuple(shape), lambda *_: (layer,) + (0,) * len(shape), **kwargs)


def _local(u, u1, n1, n2):
    in1 = u < u1
    loc = jnp.where(in1, lax.rem(u, n1), lax.rem(jnp.maximum(u - u1, 0), n2))
    return loc, jnp.where(in1, n1, n2)


def _rms(x, g):
    ms = jnp.mean(x * x, axis=-1, keepdims=True)
    return (x * lax.rsqrt(ms + EPS)) * g


def _sigmoid(x):
    return 0.5 * (jnp.tanh(0.5 * x) + 1.0)


def _gelu_tanh(x):
    c = np.float32(np.sqrt(2.0 / np.pi))
    return x * (0.5 * (1.0 + jnp.tanh(c * (x + np.float32(0.044715) * (x * x * x)))))


def _x_specs(xs, tm, u1, nblk=None):
    width = xs[0].shape[1]

    def blk(i):
        return i if nblk is None else nblk - 1 - i

    if len(xs) == 1:
        return [pl.BlockSpec((tm, width), lambda i: (blk(i), 0))]
    return [pl.BlockSpec((tm, width), lambda i: (jnp.minimum(blk(i), u1 - 1), 0)),
            pl.BlockSpec((tm, width), lambda i: (jnp.maximum(blk(i) - u1, 0), 0))]


def _read_x(x_refs, u1, blk):
    if len(x_refs) == 1:
        return x_refs[0][...]
    return jnp.where(blk < u1, x_refs[0][...], x_refs[1][...])


def _inproj_kernel(*refs, nx, u1, n1, n2):
    x_refs = refs[:nx]
    (g_ref, w_ref, cos_ref, sin_ref, cd_ref, kdb_ref,
     na_ref, rqk_ref, rv_ref, gts_ref, sb_ref, h_ref, st_ref) = refs[nx:]
    blk = pl.num_programs(0) - 1 - pl.program_id(0)
    loc, n = _local(blk, u1, n1, n2)
    h_ref[...] = _rms(_read_x(x_refs, u1, blk), g_ref[...]).astype(BF16)

    @pl.when(loc == n - 1)
    def _():
        st_ref[...] = jnp.zeros_like(st_ref)

    def proj(c0):
        return jnp.dot(h_ref[...], w_ref[:, c0:c0 + PROJ_TN], preferred_element_type=F32)

    def rope(acc, c0, scale):
        cos2 = cos_ref[...]
        sin2 = sin_ref[...]
        for hd in range(PROJ_TN // RET_QK_DIM):
            xh = acc[:, hd * RET_QK_DIM:(hd + 1) * RET_QK_DIM]
            r = xh * cos2 + pltpu.roll(xh, RET_QK_DIM // 2, axis=1) * sin2
            if scale is not None:
                r = r * scale
            rqk_ref[:, c0 + hd * RET_QK_DIM:c0 + (hd + 1) * RET_QK_DIM] = r

    def sweep_chunk(cc):
        tok = slice(cc * RET_CHUNK, (cc + 1) * RET_CHUNK)
        for h in range(RET_HEADS):
            st = st_ref[h]
            sb_ref[cc, h] = st.astype(BF16)
            k = rqk_ref[tok, RET_QK_WIDTH + h * RET_QK_DIM:RET_QK_WIDTH + (h + 1) * RET_QK_DIM]
            vh = rv_ref[tok, h * RET_V_DIM:(h + 1) * RET_V_DIM]
            st_ref[h] = st * cd_ref[1, h] + lax.dot_general(
                (k * kdb_ref[h]).astype(BF16), vh, (((0,), (0,)), ((), ())), preferred_element_type=F32)

    c_na, c_rq = 0, 3 * NA_WIDTH
    c_rk, c_rv = c_rq + RET_QK_WIDTH, c_rq + 2 * RET_QK_WIDTH
    c_gts = c_rv + RET_V_WIDTH
    rope(proj(c_rk), RET_QK_WIDTH, np.float32(RET_QK_DIM ** -0.5))
    for c0 in range(0, RET_V_WIDTH, PROJ_TN):
        rv_ref[:, c0:c0 + PROJ_TN] = proj(c_rv + c0).astype(BF16)
    pending = list(reversed(range(RET_CHUNKS_PER_STEP)))
    for c0 in range(0, 3 * NA_WIDTH, PROJ_TN):
        na_ref[:, c0:c0 + PROJ_TN] = proj(c_na + c0).astype(BF16)
        if pending:
            sweep_chunk(pending.pop(0))
    rope(proj(c_rq), 0, None)
    for c0 in range(0, RET_V_WIDTH + 2 * D_MODEL, PROJ_TN):
        if pending:
            sweep_chunk(pending.pop(0))
        z = proj(c_gts + c0)
        act = z * _sigmoid(z) if c0 < RET_V_WIDTH else _sigmoid(z)
        gts_ref[:, c0:c0 + PROJ_TN] = act.astype(BF16)


def _inproj(xs, layer, g, w, cos2, sin2, cd, kdb, tm, u1, n1, n2):
    T = sum(x.shape[0] for x in xs)
    nblk = T // tm
    cps = RET_CHUNKS_PER_STEP
    assert RET_QK_WIDTH == PROJ_TN and tm == cps * RET_CHUNK

    def tab_map(i):
        return (_local(nblk - 1 - i, u1, n1, n2)[0], 0)

    def rows(width):
        return pl.BlockSpec((tm, width), lambda i: (nblk - 1 - i, 0))

    return pl.pallas_call(
        functools.partial(_inproj_kernel, nx=len(xs), u1=u1, n1=n1, n2=n2),
        grid=(nblk,),
        in_specs=_x_specs(xs, tm, u1, nblk) + [
            _layer_spec((1, D_MODEL), layer),
            _layer_spec((D_MODEL, D_IN), layer, pipeline_mode=pl.Buffered(1)),
            pl.BlockSpec((tm, RET_QK_DIM), tab_map),
            pl.BlockSpec((tm, RET_QK_DIM), tab_map),
            pl.BlockSpec(memory_space=pltpu.SMEM),
            _layer_spec((RET_HEADS, RET_CHUNK, RET_QK_DIM), layer),
        ],
        out_specs=[
            rows(3 * NA_WIDTH), rows(2 * RET_QK_WIDTH), rows(RET_V_WIDTH), rows(RET_V_WIDTH + 2 * D_MODEL),
            pl.BlockSpec((cps, RET_HEADS, RET_QK_DIM, RET_V_DIM), lambda i: (nblk - 1 - i, 0, 0, 0)),
        ],
        out_shape=[
            jax.ShapeDtypeStruct((T, 3 * NA_WIDTH), BF16),
            jax.ShapeDtypeStruct((T, 2 * RET_QK_WIDTH), F32),
            jax.ShapeDtypeStruct((T, RET_V_WIDTH), BF16),
            jax.ShapeDtypeStruct((T, RET_V_WIDTH + 2 * D_MODEL), BF16),
            jax.ShapeDtypeStruct((nblk * cps, RET_HEADS, RET_QK_DIM, RET_V_DIM), BF16),
        ],
        scratch_shapes=[pltpu.VMEM((tm, D_MODEL), BF16),
                        pltpu.VMEM((RET_HEADS, RET_QK_DIM, RET_V_DIM), F32)],
        compiler_params=_cparams(("arbitrary",)),
        name="inproj",
    )(*xs, g, w, cos2, sin2, cd, kdb)


def _na_kernel(q_ref, k_ref, v_ref, tab_ref, o_ref, *, rows):
    pair = 2 * GRID_W
    lane = lax.broadcasted_iota(jnp.int32, (pair, LANES), 1)
    sub = lax.broadcasted_iota(jnp.int32, (pair, LANES), 0)
    keep = (lane < NA_HEAD_DIM) == (sub < GRID_W)
    low = lax.broadcasted_iota(jnp.int32, (SUBLANES, LANES), 1) < NA_HEAD_DIM
    scale = NA_HEAD_DIM ** -0.5

    def tok(row):
        return pl.ds(pl.multiple_of(row * GRID_W, GRID_W), GRID_W)

    ncb = GRID_W // NA_WW
    qcol = np.arange(GRID_W)
    qstart = np.clip(qcol - NA_WW // 2, 0, GRID_W - NA_WW)
    live = []
    for i in range(GRID_W // SUBLANES):
        lo, hi = qstart[i * SUBLANES], qstart[(i + 1) * SUBLANES - 1] + NA_WW
        live.append([j for j in range(ncb) if j * NA_WW < hi and (j + 1) * NA_WW > lo])

    def band_rows(ref, rs):
        rows_ = ref[pl.ds(pl.multiple_of(rs * GRID_W, GRID_W), NA_WH * GRID_W), :]
        return jnp.concatenate(
            [rows_[w * GRID_W + j * NA_WW:w * GRID_W + (j + 1) * NA_WW]
             for j in range(ncb) for w in range(NA_WH)], axis=0)

    def scores(r):
        rs = jnp.clip(r - NA_WH // 2, 0, rows - NA_WH)
        q = q_ref[tok(r), :] * scale
        qm = jnp.where(keep, jnp.concatenate([q, q], axis=0), jnp.zeros((pair, LANES), BF16))
        s = lax.dot_general(qm, band_rows(k_ref, rs), (((1,), (1,)), ((), ())), preferred_element_type=F32)
        d = r - rs
        tiles = {}
        for b in range(pair // SUBLANES):
            for j in live[b % (GRID_W // SUBLANES)]:
                rows8, lanes = slice(b * SUBLANES, (b + 1) * SUBLANES), slice(j * LANES, (j + 1) * LANES)
                tiles[b, j] = s[rows8, lanes] + tab_ref[0, d, rows8, lanes]
        return tiles

    def softmax(tiles):
        p, ls = {}, []
        for b in range(pair // SUBLANES):
            mine = [tiles[b, j] for j in live[b % (GRID_W // SUBLANES)]]
            m = jnp.max(functools.reduce(jnp.maximum, mine), axis=-1, keepdims=True)
            es = [jnp.exp(t - m) for t in mine]
            for j, e in zip(live[b % (GRID_W // SUBLANES)], es):
                p[b, j] = e
            ls.append(jnp.sum(functools.reduce(jnp.add, es), axis=-1, keepdims=True))
        zero = jnp.zeros((SUBLANES, LANES), F32)
        full = jnp.concatenate(
            [jnp.concatenate([p.get((b, j), zero) for j in range(ncb)], axis=1)
             for b in range(pair // SUBLANES)], axis=0)
        return full.astype(BF16), ls

    def weighted(r, p, ls):
        rs = jnp.clip(r - NA_WH // 2, 0, rows - NA_WH)
        o = jnp.dot(p, band_rows(v_ref, rs), preferred_element_type=F32)
        nb = GRID_W // SUBLANES
        out = [jnp.where(low,
                         o[i * SUBLANES:(i + 1) * SUBLANES] / ls[i],
                         o[(nb + i) * SUBLANES:(nb + i + 1) * SUBLANES] / ls[nb + i]) for i in range(nb)]
        o_ref[tok(r), :] = jnp.concatenate(out, axis=0).astype(BF16)

    group = math.gcd(rows, NA_ROW_GROUP)

    def body(g, carry):
        rs_ = [g * group + i for i in range(group)]
        ss = [scores(r) for r in rs_]
        pls = [softmax(s) for s in ss]
        for r, (p, l) in zip(rs_, pls):
            weighted(r, p, l)
        return carry

    lax.fori_loop(0, rows // group, body, 0)


def _na(na, layer, tab, tok0, nseq, S):
    rows = S // GRID_W
    b0 = tok0 // S
    npair = NA_HEADS // 2
    return pl.pallas_call(
        functools.partial(_na_kernel, rows=rows),
        grid=(nseq, npair),
        in_specs=[
            pl.BlockSpec((S, LANES), lambda b, hp: (b0 + b, hp)),
            pl.BlockSpec((S, LANES), lambda b, hp: (b0 + b, npair + hp)),
            pl.BlockSpec((S, LANES), lambda b, hp: (b0 + b, 2 * npair + hp)),
            pl.BlockSpec((None, 1, NA_WH, 2 * GRID_W, NA_WH * GRID_W), lambda b, hp: (layer, hp, 0, 0, 0)),
        ],
        out_specs=pl.BlockSpec((S, LANES), lambda b, hp: (b, hp)),
        out_shape=jax.ShapeDtypeStruct((nseq * S, NA_WIDTH), BF16),
        compiler_params=_cparams(("parallel", "parallel")),
        name="na",
    )(na, na, na, tab)


def _na_bias_table(rel_bias):
    H = rel_bias.shape[0]
    col = np.arange(GRID_W)
    col_start = np.clip(col - NA_WW // 2, 0, GRID_W - NA_WW)
    valid = (col[None, :] >= col_start[:, None]) & (col[None, :] < col_start[:, None] + NA_WW)
    rel = col[None, :] - col[:, None] + (NA_WW - 1)
    pick = (valid[:, :, None] & (rel[:, :, None] == np.arange(2 * NA_WW - 1))).astype(np.float32)
    cm = jnp.einsum('hij,ckj->hick', rel_bias.astype(F32), pick, precision=lax.Precision.HIGHEST)
    cm = jnp.where(valid[None, None], cm, MASK_VALUE)
    per_d = [jnp.transpose(cm[:, NA_WH - 1 - d:2 * NA_WH - 1 - d], (0, 2, 1, 3)) for d in range(NA_WH)]
    t = jnp.stack(per_d, axis=1)
    t = t.reshape(H, NA_WH, GRID_W, NA_WH, GRID_W // NA_WW, NA_WW).transpose(0, 1, 2, 4, 3, 5)
    t = t.reshape(H // 2, 2, NA_WH, GRID_W, NA_WH * GRID_W)
    return jnp.transpose(t, (0, 2, 1, 3, 4)).reshape(H // 2, NA_WH, 2 * GRID_W, NA_WH * GRID_W)


def _mix_kernel(*refs, nx, u1, n1, n2):
    x_refs, a_refs = refs[:nx], refs[nx:nx + 2]
    (cd_ref, qk_ref, v_ref, sb_ref, gts_ref, gn_ref, dm_ref, qdf_ref, qdb_ref, kdf_ref,
     wa_ref, wr_ref, wo_ref, x_out_ref, st_ref, o_ref) = refs[nx + 2:]
    g_attn, g_ret = RET_V_WIDTH, RET_V_WIDTH + D_MODEL
    blk = pl.program_id(0)
    loc, _ = _local(blk, u1, n1, n2)

    @pl.when(loc == 0)
    def _():
        st_ref[...] = jnp.zeros_like(st_ref)

    ya = jnp.dot(_read_x(a_refs, u1, blk), wa_ref[...], preferred_element_type=F32)

    for cc in range(RET_CHUNKS_PER_STEP):
        tok = slice(cc * RET_CHUNK, (cc + 1) * RET_CHUNK)
        for h in range(RET_HEADS):
            qh = qk_ref[tok, h * RET_QK_DIM:(h + 1) * RET_QK_DIM]
            kh = qk_ref[tok, RET_QK_WIDTH + h * RET_QK_DIM:RET_QK_WIDTH + (h + 1) * RET_QK_DIM]
            vh = v_ref[tok, h * RET_V_DIM:(h + 1) * RET_V_DIM]
            st = st_ref[h]
            s = lax.dot_general(qh.astype(BF16), kh.astype(BF16), (((1,), (1,)), ((), ())),
                                preferred_element_type=F32)
            lhs = jnp.concatenate([(s * dm_ref[h]).astype(BF16), (qh * qdf_ref[h]).astype(BF16),
                                   (qh * qdb_ref[h]).astype(BF16)], axis=1)
            rhs = jnp.concatenate([vh, st.astype(BF16), sb_ref[cc, h]], axis=0)
            o = jnp.dot(lhs, rhs, preferred_element_type=F32)
            st_ref[h] = st * cd_ref[0, h] + lax.dot_general(
                (kh * kdf_ref[h]).astype(BF16), vh, (((0,), (0,)), ((), ())), preferred_element_type=F32)
            mu = jnp.mean(o, axis=-1, keepdims=True)
            oc = o - mu
            var = jnp.mean(oc * oc, axis=-1, keepdims=True)
            y = (oc * lax.rsqrt(var + EPS)) * gn_ref[:, h * RET_V_DIM:(h + 1) * RET_V_DIM]
            swish_g = gts_ref[tok, h * RET_V_DIM:(h + 1) * RET_V_DIM].astype(F32)
            o_ref[tok, h * RET_V_DIM:(h + 1) * RET_V_DIM] = (swish_g * y).astype(BF16)

    yr = jnp.dot(o_ref[...], wr_ref[...], preferred_element_type=F32)
    mixed = (gts_ref[:, g_attn:g_attn + D_MODEL].astype(F32) * ya
             + gts_ref[:, g_ret:g_ret + D_MODEL].astype(F32) * yr)
    x_out_ref[...] = _read_x(x_refs, u1, blk) + jnp.dot(mixed.astype(BF16), wo_ref[...],
                                                       preferred_element_type=F32)


def _ret_tables(decay_f, decay_b):
    C = RET_CHUNK
    lgf = jax.nn.log_sigmoid(decay_f.astype(F32))
    lgb = jax.nn.log_sigmoid(decay_b.astype(F32))
    idx = jnp.arange(C, dtype=F32)
    diff = idx[:, None] - idx[None, :]
    dmat = jnp.where(diff[None] >= 0,
                     jnp.exp(jnp.maximum(diff, 0.0)[None] * lgf[:, None, None]),
                     jnp.exp(jnp.maximum(-diff, 0.0)[None] * lgb[:, None, None]))

    def rows(e):
        return jnp.broadcast_to(e[:, :, None], (RET_HEADS, C, RET_QK_DIM))

    qdf = rows(jnp.exp((idx + 1.0)[None, :] * lgf[:, None]))
    kdf = rows(jnp.exp((C - 1.0 - idx)[None, :] * lgf[:, None]))
    qdb = rows(jnp.exp((C - idx)[None, :] * lgb[:, None]))
    kdb = rows(jnp.exp(idx[None, :] * lgb[:, None]))
    cd = jnp.stack([jnp.exp(C * lgf), jnp.exp(C * lgb)])
    return dict(dmat=dmat, qdf=qdf, kdf=kdf, qdb=qdb, kdb=kdb, cd=cd)


def _mix(xs, a12, rqk, rv, sb, gts, layer, cd, gn, tabs, wa, wr, wo, u1, n1, n2):
    T = rqk.shape[0]
    C = RET_CHUNK
    tm = RET_CHUNKS_PER_STEP * C
    smem = pl.BlockSpec(memory_space=pltpu.SMEM)
    full3 = _layer_spec((RET_HEADS, C, RET_QK_DIM), layer)

    def rows(width, col=0):
        return pl.BlockSpec((tm, width), lambda i: (i, col))

    return pl.pallas_call(
        functools.partial(_mix_kernel, nx=len(xs), u1=u1, n1=n1, n2=n2),
        grid=(T // tm,),
        in_specs=_x_specs(xs, tm, u1) + _x_specs(a12, tm, u1) + [
            smem,
            rows(2 * RET_QK_WIDTH), rows(RET_V_WIDTH),
            pl.BlockSpec((RET_CHUNKS_PER_STEP, RET_HEADS, RET_QK_DIM, RET_V_DIM), lambda i: (i, 0, 0, 0)),
            rows(RET_V_WIDTH + 2 * D_MODEL),
            _layer_spec((1, RET_V_WIDTH), layer),
            _layer_spec((RET_HEADS, C, C), layer),
            full3, full3, full3,
            _layer_spec((NA_WIDTH, D_MODEL), layer),
            _layer_spec((RET_V_WIDTH, D_MODEL), layer),
            _layer_spec((D_MODEL, D_MODEL), layer),
        ],
        out_specs=rows(D_MODEL),
        out_shape=jax.ShapeDtypeStruct((T, D_MODEL), F32),
        scratch_shapes=[pltpu.VMEM((RET_HEADS, RET_QK_DIM, RET_V_DIM), F32),
                        pltpu.VMEM((tm, RET_V_WIDTH), BF16)],
        compiler_params=_cparams(("arbitrary",)),
        name="mix",
    )(*xs, *a12, cd, rqk, rv, sb, gts, gn,
      tabs["dmat"], tabs["qdf"], tabs["qdb"], tabs["kdf"], wa, wr, wo)


def _ffn_kernel(xp_ref, x_ref, xn_ref, g_ref, wu_ref, cw_ref, wd_ref, gf_ref, *rest, tm, u1, n1, n2, final):
    o_refs, (h_ref, a_ref, xs_ref) = rest[:-3], rest[-3:]
    sub = SUBLANES
    npl = tm // sub
    loc, n = _local(pl.program_id(0), u1, n1, n2)
    g = g_ref[...]

    xs_ref[...] = jnp.swapaxes(x_ref[...].reshape(sub, npl, D_MODEL), 0, 1).reshape(tm, D_MODEL)
    h_ref[0:tm, :] = _rms(xs_ref[...], g).astype(BF16)
    hp = jnp.where(loc == 0, 0.0, _rms(xp_ref[...], g))[sub - 1:sub]
    hn = jnp.where(loc == n - 1, 0.0, _rms(xn_ref[...], g))[0:1]
    hrow = lax.broadcasted_iota(jnp.int32, (2 * sub, D_MODEL), 0)
    h_ref[tm:tm + 2 * sub, :] = jnp.where(hrow == 0, hp, jnp.where(hrow == 1, hn, 0.0)).astype(BF16)

    def up(c0, wd):
        h = h_ref[...]
        return (jnp.dot(h, wu_ref[:, c0:c0 + wd], preferred_element_type=F32),
                jnp.dot(h, wu_ref[:, D_FF + c0:D_FF + c0 + wd], preferred_element_type=F32))

    def conv(u, c0):
        wd = u.shape[1]
        row = lax.broadcasted_iota(jnp.int32, (sub, wd), 0)
        first = jnp.where(row == 0, u[tm:tm + 1], pltpu.roll(u[tm - sub:tm], 1, axis=0))
        last = jnp.where(row == sub - 1, u[tm + 1:tm + 2], pltpu.roll(u[0:sub], sub - 1, axis=0))
        before = jnp.concatenate([first, u[0:tm - sub]], axis=0)
        after = jnp.concatenate([u[sub:tm], last], axis=0)
        return (before * cw_ref[0:1, c0:c0 + wd] + u[0:tm] * cw_ref[1:2, c0:c0 + wd]
                + after * cw_ref[2:3, c0:c0 + wd])

    tiles = [(c0, min(FFN_TN, D_FF - c0)) for c0 in range(0, D_FF, FFN_TN)]
    nxt = up(*tiles[0])
    for t, (c0, wd) in enumerate(tiles):
        ug, uv = nxt
        if t + 1 < len(tiles):
            nxt = up(*tiles[t + 1])
        a_ref[:, c0:c0 + wd] = (_gelu_tanh(conv(ug, c0)) * conv(uv, D_FF + c0)).astype(BF16)

    y = jnp.dot(a_ref[...], wd_ref[...], preferred_element_type=F32)
    y = xs_ref[...] + y
    if final:
        y = _rms(y, gf_ref[...])
    y = jnp.swapaxes(y.reshape(npl, sub, D_MODEL), 0, 1).reshape(tm, D_MODEL)
    if len(o_refs) == 1:
        o_refs[0][...] = y
    else:
        @pl.when(pl.program_id(0) < u1)
        def _():
            o_refs[0][...] = y

        @pl.when(pl.program_id(0) >= u1)
        def _():
            o_refs[1][...] = y


def _ffn(x, layer, g, w_up, conv_w, w_down, gf, tm, u1, n1, n2, final):
    T = x.shape[0]
    hb = tm // SUBLANES
    last = T // SUBLANES - 1
    resident = dict(pipeline_mode=pl.Buffered(1))
    if final:
        out_specs = [pl.BlockSpec((tm, D_MODEL), lambda i: (jnp.minimum(i, u1 - 1), 0)),
                     pl.BlockSpec((tm, D_MODEL), lambda i: (jnp.maximum(i - u1, 0), 0))]
        out_shape = [jax.ShapeDtypeStruct((u1 * tm, D_MODEL), F32),
                     jax.ShapeDtypeStruct((T - u1 * tm, D_MODEL), F32)]
    else:
        out_specs = pl.BlockSpec((tm, D_MODEL), lambda i: (i, 0))
        out_shape = jax.ShapeDtypeStruct((T, D_MODEL), F32)
    return pl.pallas_call(
        functools.partial(_ffn_kernel, tm=tm, u1=u1, n1=n1, n2=n2, final=final),
        grid=(T // tm,),
        in_specs=[
            pl.BlockSpec((SUBLANES, D_MODEL), lambda i: (jnp.maximum(i * hb - 1, 0), 0)),
            pl.BlockSpec((tm, D_MODEL), lambda i: (i, 0)),
            pl.BlockSpec((SUBLANES, D_MODEL), lambda i: (jnp.minimum((i + 1) * hb, last), 0)),
            _layer_spec((1, D_MODEL), layer),
            _layer_spec((D_MODEL, 2 * D_FF), layer, **resident),
            _layer_spec((3, 2 * D_FF), layer, **resident),
            _layer_spec((D_FF, D_MODEL), layer, **resident),
            pl.BlockSpec((1, D_MODEL), lambda i: (0, 0)),
        ],
        out_specs=out_specs,
        out_shape=out_shape,
        scratch_shapes=[
            pltpu.VMEM((tm + 2 * SUBLANES, D_MODEL), BF16),
            pltpu.VMEM((tm, D_FF), BF16),
            pltpu.VMEM((tm, D_MODEL), F32),
        ],
        compiler_params=_cparams(("arbitrary",)),
        name="ffn",
    )(x, x, x, g, w_up, conv_w, w_down, gf)


def _rope_tables(seq_len):
    inv_freq = ROPE_BASE ** (-jnp.arange(0, RET_QK_DIM, 2, dtype=F32) / RET_QK_DIM)
    ang = jnp.arange(seq_len, dtype=F32)[:, None] * inv_freq[None, :]
    cos, sin = jnp.cos(ang), jnp.sin(ang)
    return jnp.concatenate([cos, cos], axis=1), jnp.concatenate([-sin, sin], axis=1)


def _pick_tile(limit, *sizes):
    t = limit
    while any(s % t for s in sizes):
        t //= 2
    return t


def kernel(x_prompt, x_sample, norm_mix_g, w_in, na_rel_bias, ret_decay_fwd, ret_decay_bwd, ret_norm_g,
           w_branch_attn, w_branch_ret, w_out, norm_ffn_g, w_up, ffn_conv_w, w_down, norm_final_g):
    B1, S1, D = x_prompt.shape
    B2, S2, _ = x_sample.shape
    T1, T2 = B1 * S1, B2 * S2
    T = T1 + T2
    depth = w_in.shape[0]
    assert D == D_MODEL and w_in.shape[2] == D_IN
    assert S1 % (NA_WH * GRID_W) == 0 and S2 % (NA_WH * GRID_W) == 0 and T1 % S2 == 0

    xs = (x_prompt.reshape(T1, D), x_sample.reshape(T2, D))
    cos2, sin2 = _rope_tables(max(S1, S2))

    tm_proj = _pick_tile(512, S1, S2)
    tm_mix = _pick_tile(512, S1, S2)

    def units(t):
        return T1 // t, S1 // t, S2 // t

    def gain(g):
        return g.astype(F32)[:, None, :]

    w_in_b, w_up_b, w_down_b = w_in.astype(BF16), w_up.astype(BF16), w_down.astype(BF16)
    wa_b, wr_b, wo_b = w_branch_attn.astype(BF16), w_branch_ret.astype(BF16), w_out.astype(BF16)
    g_mix, g_ffn, g_ret, conv_w = gain(norm_mix_g), gain(norm_ffn_g), gain(ret_norm_g), ffn_conv_w.astype(F32)
    tabs = jax.vmap(_ret_tables)(ret_decay_fwd, ret_decay_bwd)
    heads = na_rel_bias.shape[1]
    tab = _na_bias_table(na_rel_bias.reshape((depth * heads,) + na_rel_bias.shape[2:]))
    tab = tab.reshape((depth, heads // 2) + tab.shape[1:])

    for l in range(depth):
        na, rqk, rv, gts, sb = _inproj(xs, l, g_mix, w_in_b, cos2, sin2, tabs["cd"][l], tabs["kdb"],
                                       tm_proj, *units(tm_proj))
        a = (_na(na, l, tab, 0, B1, S1), _na(na, l, tab, T1, B2, S2))
        x = _mix(xs, a, rqk, rv, sb, gts, l, tabs["cd"][l], g_ret, tabs, wa_b, wr_b, wo_b,
                 *units(RET_CHUNK * RET_CHUNKS_PER_STEP))
        x = _ffn(x, l, g_ffn, w_up_b, conv_w, w_down_b, norm_final_g[None].astype(F32), tm_mix, *units(tm_mix),
                 final=(l == depth - 1))
        xs = (x,)

    y1, y2 = x
    return y1.reshape(B1, S1, D), y2.reshape(B2, S2, D)
```

```python
import functools
import math

import numpy as np
import jax
import jax.numpy as jnp
from jax import lax
from jax.experimental import pallas as pl
from jax.experimental.pallas import tpu as pltpu

F32 = jnp.float32
BF16 = jnp.bfloat16

D_MODEL = 1024
GRID_W = 64
NA_HEADS = 8
NA_HEAD_DIM = 64
NA_WIDTH = NA_HEADS * NA_HEAD_DIM
NA_WH = 8
NA_WW = 16
RET_HEADS = 4
RET_QK_DIM = 128
RET_V_DIM = 256
RET_QK_WIDTH = RET_HEADS * RET_QK_DIM
RET_V_WIDTH = RET_HEADS * RET_V_DIM
RET_CHUNK = 256
ROPE_BASE = 10000.0
D_FF = 2816
EPS = 1e-6
D_IN = 3 * NA_WIDTH + 2 * RET_QK_WIDTH + 2 * RET_V_WIDTH + 2 * D_MODEL

LANES = 128
SUBLANES = 8
PROJ_TN = 512
FFN_TN = 256
RET_CHUNKS_PER_STEP = 2
NA_ROW_GROUP = 64
MASK_VALUE = -1e30
VMEM_LIMIT = 56 * 1024 * 1024


def _cparams(sem):
    return pltpu.CompilerParams(dimension_semantics=sem, vmem_limit_bytes=VMEM_LIMIT)


def _layer_spec(shape, layer, **kwargs):
    return pl.BlockSpec((None,) + tuple(shape), lambda *_: (layer,) + (0,) * len(shape), **kwargs)


def _local(u, u1, n1, n2):
    in1 = u < u1
    loc = jnp.where(in1, lax.rem(u, n1), lax.rem(jnp.maximum(u - u1, 0), n2))
    return loc, jnp.where(in1, n1, n2)


def _rms(x, g):
    ms = jnp.mean(x * x, axis=-1, keepdims=True)
    return (x * lax.rsqrt(ms + EPS)) * g


def _sigmoid(x):
    return 0.5 * (jnp.tanh(0.5 * x) + 1.0)


def _gelu_tanh(x):
    c = np.float32(np.sqrt(2.0 / np.pi))
    return x * (0.5 * (1.0 + jnp.tanh(c * (x + np.float32(0.044715) * (x * x * x)))))


def _x_specs(xs, tm, u1, nblk=None):
    width = xs[0].shape[1]

    def blk(i):
        return i if nblk is None else nblk - 1 - i

    if len(xs) == 1:
        return [pl.BlockSpec((tm, width), lambda i: (blk(i), 0))]
    return [pl.BlockSpec((tm, width), lambda i: (jnp.minimum(blk(i), u1 - 1), 0)),
            pl.BlockSpec((tm, width), lambda i: (jnp.maximum(blk(i) - u1, 0), 0))]


def _read_x(x_refs, u1, blk):
    if len(x_refs) == 1:
        return x_refs[0][...]
    return jnp.where(blk < u1, x_refs[0][...], x_refs[1][...])


def _inproj_kernel(*refs, nx, u1, n1, n2):
    x_refs = refs[:nx]
    (g_ref, w_ref, cos_ref, sin_ref, cd_ref, kdb_ref,
     na_ref, rqk_ref, rv_ref, gts_ref, sb_ref, h_ref, st_ref) = refs[nx:]
    blk = pl.num_programs(0) - 1 - pl.program_id(0)
    loc, n = _local(blk, u1, n1, n2)
    h_ref[...] = _rms(_read_x(x_refs, u1, blk), g_ref[...]).astype(BF16)

    @pl.when(loc == n - 1)
    def _():
        st_ref[...] = jnp.zeros_like(st_ref)

    def proj(c0):
        return jnp.dot(h_ref[...], w_ref[:, c0:c0 + PROJ_TN], preferred_element_type=F32)

    def rope(acc, c0, scale):
        cos2 = cos_ref[...]
        sin2 = sin_ref[...]
        for hd in range(PROJ_TN // RET_QK_DIM):
            xh = acc[:, hd * RET_QK_DIM:(hd + 1) * RET_QK_DIM]
            r = xh * cos2 + pltpu.roll(xh, RET_QK_DIM // 2, axis=1) * sin2
            if scale is not None:
                r = r * scale
            rqk_ref[:, c0 + hd * RET_QK_DIM:c0 + (hd + 1) * RET_QK_DIM] = r

    def sweep_chunk(cc):
        tok = slice(cc * RET_CHUNK, (cc + 1) * RET_CHUNK)
        for h in range(RET_HEADS):
            st = st_ref[h]
            sb_ref[cc, h] = st.astype(BF16)
            k = rqk_ref[tok, RET_QK_WIDTH + h * RET_QK_DIM:RET_QK_WIDTH + (h + 1) * RET_QK_DIM]
            vh = rv_ref[tok, h * RET_V_DIM:(h + 1) * RET_V_DIM]
            st_ref[h] = st * cd_ref[1, h] + lax.dot_general(
                (k * kdb_ref[h]).astype(BF16), vh, (((0,), (0,)), ((), ())), preferred_element_type=F32)

    c_na, c_rq = 0, 3 * NA_WIDTH
    c_rk, c_rv = c_rq + RET_QK_WIDTH, c_rq + 2 * RET_QK_WIDTH
    c_gts = c_rv + RET_V_WIDTH
    rope(proj(c_rk), RET_QK_WIDTH, np.float32(RET_QK_DIM ** -0.5))
    for c0 in range(0, RET_V_WIDTH, PROJ_TN):
        rv_ref[:, c0:c0 + PROJ_TN] = proj(c_rv + c0).astype(BF16)
    pending = list(reversed(range(RET_CHUNKS_PER_STEP)))
    for c0 in range(0, 3 * NA_WIDTH, PROJ_TN):
        na_ref[:, c0:c0 + PROJ_TN] = proj(c_na + c0).astype(BF16)
        if pending:
            sweep_chunk(pending.pop(0))
    rope(proj(c_rq), 0, None)
    for c0 in range(0, RET_V_WIDTH + 2 * D_MODEL, PROJ_TN):
        if pending:
            sweep_chunk(pending.pop(0))
        gts_ref[:, c0:c0 + PROJ_TN] = proj(c_gts + c0).astype(BF16)


def _inproj(xs, layer, g, w, cos2, sin2, cd, kdb, tm, u1, n1, n2):
    T = sum(x.shape[0] for x in xs)
    nblk = T // tm
    cps = RET_CHUNKS_PER_STEP
    assert RET_QK_WIDTH == PROJ_TN and tm == cps * RET_CHUNK

    def tab_map(i):
        return (_local(nblk - 1 - i, u1, n1, n2)[0], 0)

    def rows(width):
        return pl.BlockSpec((tm, width), lambda i: (nblk - 1 - i, 0))

    return pl.pallas_call(
        functools.partial(_inproj_kernel, nx=len(xs), u1=u1, n1=n1, n2=n2),
        grid=(nblk,),
        in_specs=_x_specs(xs, tm, u1, nblk) + [
            _layer_spec((1, D_MODEL), layer),
            _layer_spec((D_MODEL, D_IN), layer, pipeline_mode=pl.Buffered(1)),
            pl.BlockSpec((tm, RET_QK_DIM), tab_map),
            pl.BlockSpec((tm, RET_QK_DIM), tab_map),
            pl.BlockSpec(memory_space=pltpu.SMEM),
            _layer_spec((RET_HEADS, RET_CHUNK, RET_QK_DIM), layer),
        ],
        out_specs=[
            rows(3 * NA_WIDTH), rows(2 * RET_QK_WIDTH), rows(RET_V_WIDTH), rows(RET_V_WIDTH + 2 * D_MODEL),
            pl.BlockSpec((cps, RET_HEADS, RET_QK_DIM, RET_V_DIM), lambda i: (nblk - 1 - i, 0, 0, 0)),
        ],
        out_shape=[
            jax.ShapeDtypeStruct((T, 3 * NA_WIDTH), BF16),
            jax.ShapeDtypeStruct((T, 2 * RET_QK_WIDTH), F32),
            jax.ShapeDtypeStruct((T, RET_V_WIDTH), BF16),
            jax.ShapeDtypeStruct((T, RET_V_WIDTH + 2 * D_MODEL), BF16),
            jax.ShapeDtypeStruct((nblk * cps, RET_HEADS, RET_QK_DIM, RET_V_DIM), BF16),
        ],
        scratch_shapes=[pltpu.VMEM((tm, D_MODEL), BF16),
                        pltpu.VMEM((RET_HEADS, RET_QK_DIM, RET_V_DIM), F32)],
        compiler_params=_cparams(("arbitrary",)),
        name="inproj",
    )(*xs, g, w, cos2, sin2, cd, kdb)


def _na_kernel(q_ref, k_ref, v_ref, tab_ref, o_ref, *, rows):
    pair = 2 * GRID_W
    lane = lax.broadcasted_iota(jnp.int32, (pair, LANES), 1)
    sub = lax.broadcasted_iota(jnp.int32, (pair, LANES), 0)
    keep = (lane < NA_HEAD_DIM) == (sub < GRID_W)
    low = lax.broadcasted_iota(jnp.int32, (SUBLANES, LANES), 1) < NA_HEAD_DIM
    scale = NA_HEAD_DIM ** -0.5

    def tok(row):
        return pl.ds(pl.multiple_of(row * GRID_W, GRID_W), GRID_W)

    ncb = GRID_W // NA_WW
    qcol = np.arange(GRID_W)
    qstart = np.clip(qcol - NA_WW // 2, 0, GRID_W - NA_WW)
    live = []
    for i in range(GRID_W // SUBLANES):
        lo, hi = qstart[i * SUBLANES], qstart[(i + 1) * SUBLANES - 1] + NA_WW
        live.append([j for j in range(ncb) if j * NA_WW < hi and (j + 1) * NA_WW > lo])

    def band_rows(ref, rs):
        rows_ = ref[pl.ds(pl.multiple_of(rs * GRID_W, GRID_W), NA_WH * GRID_W), :]
        return jnp.concatenate(
            [rows_[w * GRID_W + j * NA_WW:w * GRID_W + (j + 1) * NA_WW]
             for j in range(ncb) for w in range(NA_WH)], axis=0)

    def scores(r):
        rs = jnp.clip(r - NA_WH // 2, 0, rows - NA_WH)
        q = q_ref[tok(r), :] * scale
        qm = jnp.where(keep, jnp.concatenate([q, q], axis=0), jnp.zeros((pair, LANES), BF16))
        s = lax.dot_general(qm, band_rows(k_ref, rs), (((1,), (1,)), ((), ())), preferred_element_type=F32)
        d = r - rs
        tiles = {}
        for b in range(pair // SUBLANES):
            for j in live[b % (GRID_W // SUBLANES)]:
                rows8, lanes = slice(b * SUBLANES, (b + 1) * SUBLANES), slice(j * LANES, (j + 1) * LANES)
                tiles[b, j] = s[rows8, lanes] + tab_ref[0, d, rows8, lanes]
        return tiles

    def softmax(tiles):
        p, ls = {}, []
        for b in range(pair // SUBLANES):
            mine = [tiles[b, j] for j in live[b % (GRID_W // SUBLANES)]]
            m = jnp.max(functools.reduce(jnp.maximum, mine), axis=-1, keepdims=True)
            es = [jnp.exp(t - m) for t in mine]
            for j, e in zip(live[b % (GRID_W // SUBLANES)], es):
                p[b, j] = e
            ls.append(jnp.sum(functools.reduce(jnp.add, es), axis=-1, keepdims=True))
        zero = jnp.zeros((SUBLANES, LANES), F32)
        full = jnp.concatenate(
            [jnp.concatenate([p.get((b, j), zero) for j in range(ncb)], axis=1)
             for b in range(pair // SUBLANES)], axis=0)
        return full.astype(BF16), ls

    def weighted(r, p, ls):
        rs = jnp.clip(r - NA_WH // 2, 0, rows - NA_WH)
        o = jnp.dot(p, band_rows(v_ref, rs), preferred_element_type=F32)
        nb = GRID_W // SUBLANES
        out = [jnp.where(low,
                         o[i * SUBLANES:(i + 1) * SUBLANES] / ls[i],
                         o[(nb + i) * SUBLANES:(nb + i + 1) * SUBLANES] / ls[nb + i]) for i in range(nb)]
        o_ref[tok(r), :] = jnp.concatenate(out, axis=0).astype(BF16)

    group = math.gcd(rows, NA_ROW_GROUP)

    def body(g, carry):
        rs_ = [g * group + i for i in range(group)]
        ss = [scores(r) for r in rs_]
        pls = [softmax(s) for s in ss]
        for r, (p, l) in zip(rs_, pls):
            weighted(r, p, l)
        return carry

    lax.fori_loop(0, rows // group, body, 0)


def _na(na, layer, tab, tok0, nseq, S):
    rows = S // GRID_W
    b0 = tok0 // S
    npair = NA_HEADS // 2
    return pl.pallas_call(
        functools.partial(_na_kernel, rows=rows),
        grid=(nseq, npair),
        in_specs=[
            pl.BlockSpec((S, LANES), lambda b, hp: (b0 + b, hp)),
            pl.BlockSpec((S, LANES), lambda b, hp: (b0 + b, npair + hp)),
            pl.BlockSpec((S, LANES), lambda b, hp: (b0 + b, 2 * npair + hp)),
            pl.BlockSpec((None, 1, NA_WH, 2 * GRID_W, NA_WH * GRID_W), lambda b, hp: (layer, hp, 0, 0, 0)),
        ],
        out_specs=pl.BlockSpec((S, LANES), lambda b, hp: (b, hp)),
        out_shape=jax.ShapeDtypeStruct((nseq * S, NA_WIDTH), BF16),
        compiler_params=_cparams(("parallel", "parallel")),
        name="na",
    )(na, na, na, tab)


def _na_bias_table(rel_bias):
    H = rel_bias.shape[0]
    col = np.arange(GRID_W)
    col_start = np.clip(col - NA_WW // 2, 0, GRID_W - NA_WW)
    valid = (col[None, :] >= col_start[:, None]) & (col[None, :] < col_start[:, None] + NA_WW)
    rel = col[None, :] - col[:, None] + (NA_WW - 1)
    pick = (valid[:, :, None] & (rel[:, :, None] == np.arange(2 * NA_WW - 1))).astype(np.float32)
    cm = jnp.einsum('hij,ckj->hick', rel_bias.astype(F32), pick, precision=lax.Precision.HIGHEST)
    cm = jnp.where(valid[None, None], cm, MASK_VALUE)
    per_d = [jnp.transpose(cm[:, NA_WH - 1 - d:2 * NA_WH - 1 - d], (0, 2, 1, 3)) for d in range(NA_WH)]
    t = jnp.stack(per_d, axis=1)
    t = t.reshape(H, NA_WH, GRID_W, NA_WH, GRID_W // NA_WW, NA_WW).transpose(0, 1, 2, 4, 3, 5)
    t = t.reshape(H // 2, 2, NA_WH, GRID_W, NA_WH * GRID_W)
    return jnp.transpose(t, (0, 2, 1, 3, 4)).reshape(H // 2, NA_WH, 2 * GRID_W, NA_WH * GRID_W)


def _mix_kernel(*refs, nx, u1, n1, n2):
    x_refs, a_refs = refs[:nx], refs[nx:nx + 2]
    (cd_ref, qk_ref, v_ref, sb_ref, gts_ref, gn_ref, dm_ref, qdf_ref, qdb_ref, kdf_ref,
     wa_ref, wr_ref, wo_ref, x_out_ref, st_ref, o_ref) = refs[nx + 2:]
    g_attn, g_ret = RET_V_WIDTH, RET_V_WIDTH + D_MODEL
    blk = pl.program_id(0)
    loc, _ = _local(blk, u1, n1, n2)

    @pl.when(loc == 0)
    def _():
        st_ref[...] = jnp.zeros_like(st_ref)

    ya = jnp.dot(_read_x(a_refs, u1, blk), wa_ref[...], preferred_element_type=F32)

    for cc in range(RET_CHUNKS_PER_STEP):
        tok = slice(cc * RET_CHUNK, (cc + 1) * RET_CHUNK)
        for h in range(RET_HEADS):
            qh = qk_ref[tok, h * RET_QK_DIM:(h + 1) * RET_QK_DIM]
            kh = qk_ref[tok, RET_QK_WIDTH + h * RET_QK_DIM:RET_QK_WIDTH + (h + 1) * RET_QK_DIM]
            vh = v_ref[tok, h * RET_V_DIM:(h + 1) * RET_V_DIM]
            st = st_ref[h]
            s = lax.dot_general(qh.astype(BF16), kh.astype(BF16), (((1,), (1,)), ((), ())),
                                preferred_element_type=F32)
            lhs = jnp.concatenate([(s * dm_ref[h]).astype(BF16), (qh * qdf_ref[h]).astype(BF16),
                                   (qh * qdb_ref[h]).astype(BF16)], axis=1)
            rhs = jnp.concatenate([vh, st.astype(BF16), sb_ref[cc, h]], axis=0)
            o = jnp.dot(lhs, rhs, preferred_element_type=F32)
            st_ref[h] = st * cd_ref[0, h] + lax.dot_general(
                (kh * kdf_ref[h]).astype(BF16), vh, (((0,), (0,)), ((), ())), preferred_element_type=F32)
            mu = jnp.mean(o, axis=-1, keepdims=True)
            oc = o - mu
            var = jnp.mean(oc * oc, axis=-1, keepdims=True)
            y = (oc * lax.rsqrt(var + EPS)) * gn_ref[:, h * RET_V_DIM:(h + 1) * RET_V_DIM]
            rg = gts_ref[tok, h * RET_V_DIM:(h + 1) * RET_V_DIM].astype(F32)
            o_ref[tok, h * RET_V_DIM:(h + 1) * RET_V_DIM] = ((rg * _sigmoid(rg)) * y).astype(BF16)

    yr = jnp.dot(o_ref[...], wr_ref[...], preferred_element_type=F32)
    mixed = (_sigmoid(gts_ref[:, g_attn:g_attn + D_MODEL].astype(F32)) * ya
             + _sigmoid(gts_ref[:, g_ret:g_ret + D_MODEL].astype(F32)) * yr)
    x_out_ref[...] = _read_x(x_refs, u1, blk) + jnp.dot(mixed.astype(BF16), wo_ref[...],
                                                       preferred_element_type=F32)


def _ret_tables(decay_f, decay_b):
    C = RET_CHUNK
    lgf = jax.nn.log_sigmoid(decay_f.astype(F32))
    lgb = jax.nn.log_sigmoid(decay_b.astype(F32))
    idx = jnp.arange(C, dtype=F32)
    diff = idx[:, None] - idx[None, :]
    dmat = jnp.where(diff[None] >= 0,
                     jnp.exp(jnp.maximum(diff, 0.0)[None] * lgf[:, None, None]),
                     jnp.exp(jnp.maximum(-diff, 0.0)[None] * lgb[:, None, None]))

    def rows(e):
        return jnp.broadcast_to(e[:, :, None], (RET_HEADS, C, RET_QK_DIM))

    qdf = rows(jnp.exp((idx + 1.0)[None, :] * lgf[:, None]))
    kdf = rows(jnp.exp((C - 1.0 - idx)[None, :] * lgf[:, None]))
    qdb = rows(jnp.exp((C - idx)[None, :] * lgb[:, None]))
    kdb = rows(jnp.exp(idx[None, :] * lgb[:, None]))
    cd = jnp.stack([jnp.exp(C * lgf), jnp.exp(C * lgb)])
    return dict(dmat=dmat, qdf=qdf, kdf=kdf, qdb=qdb, kdb=kdb, cd=cd)


def _mix(xs, a12, rqk, rv, sb, gts, layer, cd, gn, tabs, wa, wr, wo, u1, n1, n2):
    T = rqk.shape[0]
    C = RET_CHUNK
    tm = RET_CHUNKS_PER_STEP * C
    smem = pl.BlockSpec(memory_space=pltpu.SMEM)
    full3 = _layer_spec((RET_HEADS, C, RET_QK_DIM), layer)

    def rows(width, col=0):
        return pl.BlockSpec((tm, width), lambda i: (i, col))

    return pl.pallas_call(
        functools.partial(_mix_kernel, nx=len(xs), u1=u1, n1=n1, n2=n2),
        grid=(T // tm,),
        in_specs=_x_specs(xs, tm, u1) + _x_specs(a12, tm, u1) + [
            smem,
            rows(2 * RET_QK_WIDTH), rows(RET_V_WIDTH),
            pl.BlockSpec((RET_CHUNKS_PER_STEP, RET_HEADS, RET_QK_DIM, RET_V_DIM), lambda i: (i, 0, 0, 0)),
            rows(RET_V_WIDTH + 2 * D_MODEL),
            _layer_spec((1, RET_V_WIDTH), layer),
            _layer_spec((RET_HEADS, C, C), layer),
            full3, full3, full3,
            _layer_spec((NA_WIDTH, D_MODEL), layer),
            _layer_spec((RET_V_WIDTH, D_MODEL), layer),
            _layer_spec((D_MODEL, D_MODEL), layer),
        ],
        out_specs=rows(D_MODEL),
        out_shape=jax.ShapeDtypeStruct((T, D_MODEL), F32),
        scratch_shapes=[pltpu.VMEM((RET_HEADS, RET_QK_DIM, RET_V_DIM), F32),
                        pltpu.VMEM((tm, RET_V_WIDTH), BF16)],
        compiler_params=_cparams(("arbitrary",)),
        name="mix",
    )(*xs, *a12, cd, rqk, rv, sb, gts, gn,
      tabs["dmat"], tabs["qdf"], tabs["qdb"], tabs["kdf"], wa, wr, wo)


def _ffn_kernel(xp_ref, x_ref, xn_ref, g_ref, wu_ref, cw_ref, wd_ref, gf_ref, *rest, tm, u1, n1, n2, final):
    o_refs, (h_ref, a_ref, xs_ref) = rest[:-3], rest[-3:]
    sub = SUBLANES
    npl = tm // sub
    loc, n = _local(pl.program_id(0), u1, n1, n2)
    g = g_ref[...]

    xs_ref[...] = jnp.swapaxes(x_ref[...].reshape(sub, npl, D_MODEL), 0, 1).reshape(tm, D_MODEL)
    h_ref[0:tm, :] = _rms(xs_ref[...], g).astype(BF16)
    hp = jnp.where(loc == 0, 0.0, _rms(xp_ref[...], g))[sub - 1:sub]
    hn = jnp.where(loc == n - 1, 0.0, _rms(xn_ref[...], g))[0:1]
    hrow = lax.broadcasted_iota(jnp.int32, (2 * sub, D_MODEL), 0)
    h_ref[tm:tm + 2 * sub, :] = jnp.where(hrow == 0, hp, jnp.where(hrow == 1, hn, 0.0)).astype(BF16)

    def up(c0, wd):
        h = h_ref[...]
        return (jnp.dot(h, wu_ref[:, c0:c0 + wd], preferred_element_type=F32),
                jnp.dot(h, wu_ref[:, D_FF + c0:D_FF + c0 + wd], preferred_element_type=F32))

    def conv(u, c0):
        wd = u.shape[1]
        row = lax.broadcasted_iota(jnp.int32, (sub, wd), 0)
        first = jnp.where(row == 0, u[tm:tm + 1], pltpu.roll(u[tm - sub:tm], 1, axis=0))
        last = jnp.where(row == sub - 1, u[tm + 1:tm + 2], pltpu.roll(u[0:sub], sub - 1, axis=0))
        before = jnp.concatenate([first, u[0:tm - sub]], axis=0)
        after = jnp.concatenate([u[sub:tm], last], axis=0)
        return (before * cw_ref[0:1, c0:c0 + wd] + u[0:tm] * cw_ref[1:2, c0:c0 + wd]
                + after * cw_ref[2:3, c0:c0 + wd])

    tiles = [(c0, min(FFN_TN, D_FF - c0)) for c0 in range(0, D_FF, FFN_TN)]
    nxt = up(*tiles[0])
    for t, (c0, wd) in enumerate(tiles):
        ug, uv = nxt
        if t + 1 < len(tiles):
            nxt = up(*tiles[t + 1])
        a_ref[:, c0:c0 + wd] = (_gelu_tanh(conv(ug, c0)) * conv(uv, D_FF + c0)).astype(BF16)

    y = jnp.dot(a_ref[...], wd_ref[...], preferred_element_type=F32)
    y = xs_ref[...] + y
    if final:
        y = _rms(y, gf_ref[...])
    y = jnp.swapaxes(y.reshape(npl, sub, D_MODEL), 0, 1).reshape(tm, D_MODEL)
    if len(o_refs) == 1:
        o_refs[0][...] = y
    else:
        @pl.when(pl.program_id(0) < u1)
        def _():
            o_refs[0][...] = y

        @pl.when(pl.program_id(0) >= u1)
        def _():
            o_refs[1][...] = y


def _ffn(x, layer, g, w_up, conv_w, w_down, gf, tm, u1, n1, n2, final):
    T = x.shape[0]
    hb = tm // SUBLANES
    last = T // SUBLANES - 1
    resident = dict(pipeline_mode=pl.Buffered(1))
    if final:
        out_specs = [pl.BlockSpec((tm, D_MODEL), lambda i: (jnp.minimum(i, u1 - 1), 0)),
                     pl.BlockSpec((tm, D_MODEL), lambda i: (jnp.maximum(i - u1, 0), 0))]
        out_shape = [jax.ShapeDtypeStruct((u1 * tm, D_MODEL), F32),
                     jax.ShapeDtypeStruct((T - u1 * tm, D_MODEL), F32)]
    else:
        out_specs = pl.BlockSpec((tm, D_MODEL), lambda i: (i, 0))
        out_shape = jax.ShapeDtypeStruct((T, D_MODEL), F32)
    return pl.pallas_call(
        functools.partial(_ffn_kernel, tm=tm, u1=u1, n1=n1, n2=n2, final=final),
        grid=(T // tm,),
        in_specs=[
            pl.BlockSpec((SUBLANES, D_MODEL), lambda i: (jnp.maximum(i * hb - 1, 0), 0)),
            pl.BlockSpec((tm, D_MODEL), lambda i: (i, 0)),
            pl.BlockSpec((SUBLANES, D_MODEL), lambda i: (jnp.minimum((i + 1) * hb, last), 0)),
            _layer_spec((1, D_MODEL), layer),
            _layer_spec((D_MODEL, 2 * D_FF), layer, **resident),
            _layer_spec((3, 2 * D_FF), layer, **resident),
            _layer_spec((D_FF, D_MODEL), layer, **resident),
            pl.BlockSpec((1, D_MODEL), lambda i: (0, 0)),
        ],
        out_specs=out_specs,
        out_shape=out_shape,
        scratch_shapes=[
            pltpu.VMEM((tm + 2 * SUBLANES, D_MODEL), BF16),
            pltpu.VMEM((tm, D_FF), BF16),
            pltpu.VMEM((tm, D_MODEL), F32),
        ],
        compiler_params=_cparams(("arbitrary",)),
        name="ffn",
    )(x, x, x, g, w_up, conv_w, w_down, gf)


def _rope_tables(seq_len):
    inv_freq = ROPE_BASE ** (-jnp.arange(0, RET_QK_DIM, 2, dtype=F32) / RET_QK_DIM)
    ang = jnp.arange(seq_len, dtype=F32)[:, None] * inv_freq[None, :]
    cos, sin = jnp.cos(ang), jnp.sin(ang)
    return jnp.concatenate([cos, cos], axis=1), jnp.concatenate([-sin, sin], axis=1)


def _pick_tile(limit, *sizes):
    t = limit
    while any(s % t for s in sizes):
        t //= 2
    return t


def kernel(x_prompt, x_sample, norm_mix_g, w_in, na_rel_bias, ret_decay_fwd, ret_decay_bwd, ret_norm_g,
           w_branch_attn, w_branch_ret, w_out, norm_ffn_g, w_up, ffn_conv_w, w_down, norm_final_g):
    B1, S1, D = x_prompt.shape
    B2, S2, _ = x_sample.shape
    T1, T2 = B1 * S1, B2 * S2
    T = T1 + T2
    depth = w_in.shape[0]
    assert D == D_MODEL and w_in.shape[2] == D_IN
    assert S1 % (NA_WH * GRID_W) == 0 and S2 % (NA_WH * GRID_W) == 0 and T1 % S2 == 0

    xs = (x_prompt.reshape(T1, D), x_sample.reshape(T2, D))
    cos2, sin2 = _rope_tables(max(S1, S2))

    tm_proj = _pick_tile(512, S1, S2)
    tm_mix = _pick_tile(512, S1, S2)

    def units(t):
        return T1 // t, S1 // t, S2 // t

    def gain(g):
        return g.astype(F32)[:, None, :]

    w_in_b, w_up_b, w_down_b = w_in.astype(BF16), w_up.astype(BF16), w_down.astype(BF16)
    wa_b, wr_b, wo_b = w_branch_attn.astype(BF16), w_branch_ret.astype(BF16), w_out.astype(BF16)
    g_mix, g_ffn, g_ret, conv_w = gain(norm_mix_g), gain(norm_ffn_g), gain(ret_norm_g), ffn_conv_w.astype(F32)
    tabs = jax.vmap(_ret_tables)(ret_decay_fwd, ret_decay_bwd)
    heads = na_rel_bias.shape[1]
    tab = _na_bias_table(na_rel_bias.reshape((depth * heads,) + na_rel_bias.shape[2:]))
    tab = tab.reshape((depth, heads // 2) + tab.shape[1:])

    for l in range(depth):
        na, rqk, rv, gts, sb = _inproj(xs, l, g_mix, w_in_b, cos2, sin2, tabs["cd"][l], tabs["kdb"],
                                       tm_proj, *units(tm_proj))
        a = (_na(na, l, tab, 0, B1, S1), _na(na, l, tab, T1, B2, S2))
        x = _mix(xs, a, rqk, rv, sb, gts, l, tabs["cd"][l], g_ret, tabs, wa_b, wr_b, wo_b,
                 *units(RET_CHUNK * RET_CHUNKS_PER_STEP))
        x = _ffn(x, l, g_ffn, w_up_b, conv_w, w_down_b, norm_final_g[None].astype(F32), tm_mix, *units(tm_mix),
                 final=(l == depth - 1))
        xs = (x,)

    y1, y2 = x
    return y1.reshape(B1, S1, D), y2.reshape(B2, S2, D)
```

```python
import functools
import math

import numpy as np
import jax
import jax.numpy as jnp
from jax import lax
from jax.experimental import pallas as pl
from jax.experimental.pallas import tpu as pltpu

F32 = jnp.float32
BF16 = jnp.bfloat16

D_MODEL = 1024
GRID_W = 64
NA_HEADS = 8
NA_HEAD_DIM = 64
NA_WIDTH = NA_HEADS * NA_HEAD_DIM
NA_WH = 8
NA_WW = 16
RET_HEADS = 4
RET_QK_DIM = 128
RET_V_DIM = 256
RET_QK_WIDTH = RET_HEADS * RET_QK_DIM
RET_V_WIDTH = RET_HEADS * RET_V_DIM
RET_CHUNK = 256
ROPE_BASE = 10000.0
D_FF = 2816
EPS = 1e-6
D_IN = 3 * NA_WIDTH + 2 * RET_QK_WIDTH + 2 * RET_V_WIDTH + 2 * D_MODEL

LANES = 128
SUBLANES = 8
PROJ_TN = 512
FFN_TN = 256
RET_CHUNKS_PER_STEP = 2
NA_ROW_GROUP = 64
MASK_VALUE = -1e30
VMEM_LIMIT = 56 * 1024 * 1024


def _cparams(sem):
    return pltpu.CompilerParams(dimension_semantics=sem, vmem_limit_bytes=VMEM_LIMIT)


def _layer_spec(shape, layer, **kwargs):
    return pl.BlockSpec((None,) + tuple(shape), lambda *_: (layer,) + (0,) * len(shape), **kwargs)


def _local(u, u1, n1, n2):
    in1 = u < u1
    loc = jnp.where(in1, lax.rem(u, n1), lax.rem(jnp.maximum(u - u1, 0), n2))
    return loc, jnp.where(in1, n1, n2)


def _rms(x, g):
    ms = jnp.mean(x * x, axis=-1, keepdims=True)
    return (x * lax.rsqrt(ms + EPS)) * g


def _sigmoid(x):
    return 0.5 * (jnp.tanh(0.5 * x) + 1.0)


def _gelu_tanh(x):
    c = np.float32(np.sqrt(2.0 / np.pi))
    return x * (0.5 * (1.0 + jnp.tanh(c * (x + np.float32(0.044715) * (x * x * x)))))


def _x_specs(xs, tm, u1, nblk=None):
    width = xs[0].shape[1]

    def blk(i):
        return i if nblk is None else nblk - 1 - i

    if len(xs) == 1:
        return [pl.BlockSpec((tm, width), lambda i: (blk(i), 0))]
    return [pl.BlockSpec((tm, width), lambda i: (jnp.minimum(blk(i), u1 - 1), 0)),
            pl.BlockSpec((tm, width), lambda i: (jnp.maximum(blk(i) - u1, 0), 0))]


def _read_x(x_refs, u1, blk):
    if len(x_refs) == 1:
        return x_refs[0][...]
    return jnp.where(blk < u1, x_refs[0][...], x_refs[1][...])


def _inproj_kernel(*refs, nx, u1, n1, n2):
    x_refs = refs[:nx]
    (g_ref, w_ref, cos_ref, sin_ref, cd_ref, kdb_ref,
     na_ref, rqk_ref, rv_ref, gts_ref, sb_ref, h_ref, st_ref) = refs[nx:]
    blk = pl.num_programs(0) - 1 - pl.program_id(0)
    loc, n = _local(blk, u1, n1, n2)
    h_ref[...] = _rms(_read_x(x_refs, u1, blk), g_ref[...]).astype(BF16)

    @pl.when(loc == n - 1)
    def _():
        st_ref[...] = jnp.zeros_like(st_ref)

    def proj(c0):
        return jnp.dot(h_ref[...], w_ref[:, c0:c0 + PROJ_TN], preferred_element_type=F32)

    def rope(acc, c0, scale):
        cos2 = cos_ref[...]
        sin2 = sin_ref[...]
        for hd in range(PROJ_TN // RET_QK_DIM):
            xh = acc[:, hd * RET_QK_DIM:(hd + 1) * RET_QK_DIM]
            r = xh * cos2 + pltpu.roll(xh, RET_QK_DIM // 2, axis=1) * sin2
            if scale is not None:
                r = r * scale
            rqk_ref[:, c0 + hd * RET_QK_DIM:c0 + (hd + 1) * RET_QK_DIM] = r

    def sweep_chunk(cc):
        tok = slice(cc * RET_CHUNK, (cc + 1) * RET_CHUNK)
        for h in range(RET_HEADS):
            st = st_ref[h]
            sb_ref[cc, h] = st.astype(BF16)
            k = rqk_ref[tok, RET_QK_WIDTH + h * RET_QK_DIM:RET_QK_WIDTH + (h + 1) * RET_QK_DIM]
            vh = rv_ref[tok, h * RET_V_DIM:(h + 1) * RET_V_DIM]
            st_ref[h] = st * cd_ref[1, h] + lax.dot_general(
                (k * kdb_ref[h]).astype(BF16), vh, (((0,), (0,)), ((), ())), preferred_element_type=F32)

    c_na, c_rq = 0, 3 * NA_WIDTH
    c_rk, c_rv = c_rq + RET_QK_WIDTH, c_rq + 2 * RET_QK_WIDTH
    c_gts = c_rv + RET_V_WIDTH
    rope(proj(c_rk), RET_QK_WIDTH, np.float32(RET_QK_DIM ** -0.5))
    for c0 in range(0, RET_V_WIDTH, PROJ_TN):
        rv_ref[:, c0:c0 + PROJ_TN] = proj(c_rv + c0).astype(BF16)
    pending = list(reversed(range(RET_CHUNKS_PER_STEP)))
    for c0 in range(0, 3 * NA_WIDTH, PROJ_TN):
        na_ref[:, c0:c0 + PROJ_TN] = proj(c_na + c0).astype(BF16)
        if pending:
            sweep_chunk(pending.pop(0))
    rope(proj(c_rq), 0, None)
    for c0 in range(0, RET_V_WIDTH + 2 * D_MODEL, PROJ_TN):
        if pending:
            sweep_chunk(pending.pop(0))
        gts_ref[:, c0:c0 + PROJ_TN] = proj(c_gts + c0).astype(BF16)


def _inproj(xs, layer, g, w, cos2, sin2, cd, kdb, tm, u1, n1, n2):
    T = sum(x.shape[0] for x in xs)
    nblk = T // tm
    cps = RET_CHUNKS_PER_STEP
    assert RET_QK_WIDTH == PROJ_TN and tm == cps * RET_CHUNK

    def tab_map(i):
        return (_local(nblk - 1 - i, u1, n1, n2)[0], 0)

    def rows(width):
        return pl.BlockSpec((tm, width), lambda i: (nblk - 1 - i, 0))

    return pl.pallas_call(
        functools.partial(_inproj_kernel, nx=len(xs), u1=u1, n1=n1, n2=n2),
        grid=(nblk,),
        in_specs=_x_specs(xs, tm, u1, nblk) + [
            _layer_spec((1, D_MODEL), layer),
            _layer_spec((D_MODEL, D_IN), layer, pipeline_mode=pl.Buffered(1)),
            pl.BlockSpec((tm, RET_QK_DIM), tab_map),
            pl.BlockSpec((tm, RET_QK_DIM), tab_map),
            pl.BlockSpec(memory_space=pltpu.SMEM),
            _layer_spec((RET_HEADS, RET_CHUNK, RET_QK_DIM), layer),
        ],
        out_specs=[
            rows(3 * NA_WIDTH), rows(2 * RET_QK_WIDTH), rows(RET_V_WIDTH), rows(RET_V_WIDTH + 2 * D_MODEL),
            pl.BlockSpec((cps, RET_HEADS, RET_QK_DIM, RET_V_DIM), lambda i: (nblk - 1 - i, 0, 0, 0)),
        ],
        out_shape=[
            jax.ShapeDtypeStruct((T, 3 * NA_WIDTH), BF16),
            jax.ShapeDtypeStruct((T, 2 * RET_QK_WIDTH), F32),
            jax.ShapeDtypeStruct((T, RET_V_WIDTH), BF16),
            jax.ShapeDtypeStruct((T, RET_V_WIDTH + 2 * D_MODEL), BF16),
            jax.ShapeDtypeStruct((nblk * cps, RET_HEADS, RET_QK_DIM, RET_V_DIM), BF16),
        ],
        scratch_shapes=[pltpu.VMEM((tm, D_MODEL), BF16),
                        pltpu.VMEM((RET_HEADS, RET_QK_DIM, RET_V_DIM), F32)],
        compiler_params=_cparams(("arbitrary",)),
        name="inproj",
    )(*xs, g, w, cos2, sin2, cd, kdb)


def _na_kernel(q_ref, k_ref, v_ref, tab_ref, o_ref, *, rows):
    pair = 2 * GRID_W
    lane = lax.broadcasted_iota(jnp.int32, (pair, LANES), 1)
    sub = lax.broadcasted_iota(jnp.int32, (pair, LANES), 0)
    keep = (lane < NA_HEAD_DIM) == (sub < GRID_W)
    low = lax.broadcasted_iota(jnp.int32, (SUBLANES, LANES), 1) < NA_HEAD_DIM
    scale = NA_HEAD_DIM ** -0.5

    def tok(row):
        return pl.ds(pl.multiple_of(row * GRID_W, GRID_W), GRID_W)

    ncb = GRID_W // NA_WW
    qcol = np.arange(GRID_W)
    qstart = np.clip(qcol - NA_WW // 2, 0, GRID_W - NA_WW)
    live = []
    for i in range(GRID_W // SUBLANES):
        lo, hi = qstart[i * SUBLANES], qstart[(i + 1) * SUBLANES - 1] + NA_WW
        live.append([j for j in range(ncb) if j * NA_WW < hi and (j + 1) * NA_WW > lo])

    def band_rows(ref, rs):
        rows_ = ref[pl.ds(pl.multiple_of(rs * GRID_W, GRID_W), NA_WH * GRID_W), :]
        return jnp.concatenate(
            [rows_[w * GRID_W + j * NA_WW:w * GRID_W + (j + 1) * NA_WW]
             for j in range(ncb) for w in range(NA_WH)], axis=0)

    def scores(r):
        rs = jnp.clip(r - NA_WH // 2, 0, rows - NA_WH)
        q = q_ref[tok(r), :] * scale
        qm = jnp.where(keep, jnp.concatenate([q, q], axis=0), jnp.zeros((pair, LANES), BF16))
        s = lax.dot_general(qm, band_rows(k_ref, rs), (((1,), (1,)), ((), ())), preferred_element_type=F32)
        d = r - rs
        tiles = {}
        for b in range(pair // SUBLANES):
            for j in live[b % (GRID_W // SUBLANES)]:
                rows8, lanes = slice(b * SUBLANES, (b + 1) * SUBLANES), slice(j * LANES, (j + 1) * LANES)
                tiles[b, j] = s[rows8, lanes] + tab_ref[0, d, rows8, lanes]
        return tiles

    def softmax(tiles):
        p, ls = {}, []
        for b in range(pair // SUBLANES):
            mine = [tiles[b, j] for j in live[b % (GRID_W // SUBLANES)]]
            m = jnp.max(functools.reduce(jnp.maximum, mine), axis=-1, keepdims=True)
            es = [jnp.exp(t - m) for t in mine]
            for j, e in zip(live[b % (GRID_W // SUBLANES)], es):
                p[b, j] = e
            ls.append(jnp.sum(functools.reduce(jnp.add, es), axis=-1, keepdims=True))
        zero = jnp.zeros((SUBLANES, LANES), F32)
        full = jnp.concatenate(
            [jnp.concatenate([p.get((b, j), zero) for j in range(ncb)], axis=1)
             for b in range(pair // SUBLANES)], axis=0)
        return full.astype(BF16), ls

    def weighted(r, p, ls):
        rs = jnp.clip(r - NA_WH // 2, 0, rows - NA_WH)
        o = jnp.dot(p, band_rows(v_ref, rs), preferred_element_type=F32)
        nb = GRID_W // SUBLANES
        out = [jnp.where(low,
                         o[i * SUBLANES:(i + 1) * SUBLANES] / ls[i],
                         o[(nb + i) * SUBLANES:(nb + i + 1) * SUBLANES] / ls[nb + i]) for i in range(nb)]
        o_ref[tok(r), :] = jnp.concatenate(out, axis=0).astype(BF16)

    group = math.gcd(rows, NA_ROW_GROUP)

    def body(g, carry):
        rs_ = [g * group + i for i in range(group)]
        ss = [scores(r) for r in rs_]
        pls = [softmax(s) for s in ss]
        for r, (p, l) in zip(rs_, pls):
            weighted(r, p, l)
        return carry

    lax.fori_loop(0, rows // group, body, 0)


def _na(na, layer, tab, tok0, nseq, S):
    rows = S // GRID_W
    b0 = tok0 // S
    npair = NA_HEADS // 2
    return pl.pallas_call(
        functools.partial(_na_kernel, rows=rows),
        grid=(npair, nseq),
        in_specs=[
            pl.BlockSpec((S, LANES), lambda hp, b: (b0 + b, hp)),
            pl.BlockSpec((S, LANES), lambda hp, b: (b0 + b, npair + hp)),
            pl.BlockSpec((S, LANES), lambda hp, b: (b0 + b, 2 * npair + hp)),
            pl.BlockSpec((None, 1, NA_WH, 2 * GRID_W, NA_WH * GRID_W), lambda hp, b: (layer, hp, 0, 0, 0)),
        ],
        out_specs=pl.BlockSpec((S, LANES), lambda hp, b: (b, hp)),
        out_shape=jax.ShapeDtypeStruct((nseq * S, NA_WIDTH), BF16),
        compiler_params=_cparams(("parallel", "parallel")),
        name="na",
    )(na, na, na, tab)


def _na_bias_table(rel_bias):
    H = rel_bias.shape[0]
    col = np.arange(GRID_W)
    col_start = np.clip(col - NA_WW // 2, 0, GRID_W - NA_WW)
    valid = (col[None, :] >= col_start[:, None]) & (col[None, :] < col_start[:, None] + NA_WW)
    rel = col[None, :] - col[:, None] + (NA_WW - 1)
    pick = (valid[:, :, None] & (rel[:, :, None] == np.arange(2 * NA_WW - 1))).astype(np.float32)
    cm = jnp.einsum('hij,ckj->hick', rel_bias.astype(F32), pick, precision=lax.Precision.HIGHEST)
    cm = jnp.where(valid[None, None], cm, MASK_VALUE)
    per_d = [jnp.transpose(cm[:, NA_WH - 1 - d:2 * NA_WH - 1 - d], (0, 2, 1, 3)) for d in range(NA_WH)]
    t = jnp.stack(per_d, axis=1)
    t = t.reshape(H, NA_WH, GRID_W, NA_WH, GRID_W // NA_WW, NA_WW).transpose(0, 1, 2, 4, 3, 5)
    t = t.reshape(H // 2, 2, NA_WH, GRID_W, NA_WH * GRID_W)
    return jnp.transpose(t, (0, 2, 1, 3, 4)).reshape(H // 2, NA_WH, 2 * GRID_W, NA_WH * GRID_W)


def _mix_kernel(*refs, nx, u1, n1, n2):
    x_refs, a_refs = refs[:nx], refs[nx:nx + 2]
    (cd_ref, qk_ref, v_ref, sb_ref, gts_ref, gn_ref, dm_ref, qdf_ref, qdb_ref, kdf_ref,
     wa_ref, wr_ref, wo_ref, x_out_ref, st_ref, o_ref) = refs[nx + 2:]
    g_attn, g_ret = RET_V_WIDTH, RET_V_WIDTH + D_MODEL
    blk = pl.program_id(0)
    loc, _ = _local(blk, u1, n1, n2)

    @pl.when(loc == 0)
    def _():
        st_ref[...] = jnp.zeros_like(st_ref)

    ya = jnp.dot(_read_x(a_refs, u1, blk), wa_ref[...], preferred_element_type=F32)

    for cc in range(RET_CHUNKS_PER_STEP):
        tok = slice(cc * RET_CHUNK, (cc + 1) * RET_CHUNK)
        for h in range(RET_HEADS):
            qh = qk_ref[tok, h * RET_QK_DIM:(h + 1) * RET_QK_DIM]
            kh = qk_ref[tok, RET_QK_WIDTH + h * RET_QK_DIM:RET_QK_WIDTH + (h + 1) * RET_QK_DIM]
            vh = v_ref[tok, h * RET_V_DIM:(h + 1) * RET_V_DIM]
            st = st_ref[h]
            s = lax.dot_general(qh.astype(BF16), kh.astype(BF16), (((1,), (1,)), ((), ())),
                                preferred_element_type=F32)
            o = jnp.dot((s * dm_ref[h]).astype(BF16), vh, preferred_element_type=F32)
            o = o + jnp.dot((qh * qdf_ref[h]).astype(BF16), st.astype(BF16), preferred_element_type=F32)
            o = o + jnp.dot((qh * qdb_ref[h]).astype(BF16), sb_ref[cc, h], preferred_element_type=F32)
            st_ref[h] = st * cd_ref[0, h] + lax.dot_general(
                (kh * kdf_ref[h]).astype(BF16), vh, (((0,), (0,)), ((), ())), preferred_element_type=F32)
            mu = jnp.mean(o, axis=-1, keepdims=True)
            oc = o - mu
            var = jnp.mean(oc * oc, axis=-1, keepdims=True)
            y = (oc * lax.rsqrt(var + EPS)) * gn_ref[:, h * RET_V_DIM:(h + 1) * RET_V_DIM]
            rg = gts_ref[tok, h * RET_V_DIM:(h + 1) * RET_V_DIM].astype(F32)
            o_ref[tok, h * RET_V_DIM:(h + 1) * RET_V_DIM] = ((rg * _sigmoid(rg)) * y).astype(BF16)

    yr = jnp.dot(o_ref[...], wr_ref[...], preferred_element_type=F32)
    mixed = (_sigmoid(gts_ref[:, g_attn:g_attn + D_MODEL].astype(F32)) * ya
             + _sigmoid(gts_ref[:, g_ret:g_ret + D_MODEL].astype(F32)) * yr)
    x_out_ref[...] = _read_x(x_refs, u1, blk) + jnp.dot(mixed.astype(BF16), wo_ref[...],
                                                       preferred_element_type=F32)


def _ret_tables(decay_f, decay_b):
    C = RET_CHUNK
    lgf = jax.nn.log_sigmoid(decay_f.astype(F32))
    lgb = jax.nn.log_sigmoid(decay_b.astype(F32))
    idx = jnp.arange(C, dtype=F32)
    diff = idx[:, None] - idx[None, :]
    dmat = jnp.where(diff[None] >= 0,
                     jnp.exp(jnp.maximum(diff, 0.0)[None] * lgf[:, None, None]),
                     jnp.exp(jnp.maximum(-diff, 0.0)[None] * lgb[:, None, None]))

    def rows(e):
        return jnp.broadcast_to(e[:, :, None], (RET_HEADS, C, RET_QK_DIM))

    qdf = rows(jnp.exp((idx + 1.0)[None, :] * lgf[:, None]))
    kdf = rows(jnp.exp((C - 1.0 - idx)[None, :] * lgf[:, None]))
    qdb = rows(jnp.exp((C - idx)[None, :] * lgb[:, None]))
    kdb = rows(jnp.exp(idx[None, :] * lgb[:, None]))
    cd = jnp.stack([jnp.exp(C * lgf), jnp.exp(C * lgb)])
    return dict(dmat=dmat, qdf=qdf, kdf=kdf, qdb=qdb, kdb=kdb, cd=cd)


def _mix(xs, a12, rqk, rv, sb, gts, layer, cd, gn, tabs, wa, wr, wo, u1, n1, n2):
    T = rqk.shape[0]
    C = RET_CHUNK
    tm = RET_CHUNKS_PER_STEP * C
    smem = pl.BlockSpec(memory_space=pltpu.SMEM)
    full3 = _layer_spec((RET_HEADS, C, RET_QK_DIM), layer)

    def rows(width, col=0):
        return pl.BlockSpec((tm, width), lambda i: (i, col))

    return pl.pallas_call(
        functools.partial(_mix_kernel, nx=len(xs), u1=u1, n1=n1, n2=n2),
        grid=(T // tm,),
        in_specs=_x_specs(xs, tm, u1) + _x_specs(a12, tm, u1) + [
            smem,
            rows(2 * RET_QK_WIDTH), rows(RET_V_WIDTH),
            pl.BlockSpec((RET_CHUNKS_PER_STEP, RET_HEADS, RET_QK_DIM, RET_V_DIM), lambda i: (i, 0, 0, 0)),
            rows(RET_V_WIDTH + 2 * D_MODEL),
            _layer_spec((1, RET_V_WIDTH), layer),
            _layer_spec((RET_HEADS, C, C), layer),
            full3, full3, full3,
            _layer_spec((NA_WIDTH, D_MODEL), layer),
            _layer_spec((RET_V_WIDTH, D_MODEL), layer),
            _layer_spec((D_MODEL, D_MODEL), layer),
        ],
        out_specs=rows(D_MODEL),
        out_shape=jax.ShapeDtypeStruct((T, D_MODEL), F32),
        scratch_shapes=[pltpu.VMEM((RET_HEADS, RET_QK_DIM, RET_V_DIM), F32),
                        pltpu.VMEM((tm, RET_V_WIDTH), BF16)],
        compiler_params=_cparams(("arbitrary",)),
        name="mix",
    )(*xs, *a12, cd, rqk, rv, sb, gts, gn,
      tabs["dmat"], tabs["qdf"], tabs["qdb"], tabs["kdf"], wa, wr, wo)


def _ffn_kernel(xp_ref, x_ref, xn_ref, g_ref, wu_ref, cw_ref, wd_ref, gf_ref, *rest, tm, u1, n1, n2, final):
    o_refs, (h_ref, a_ref, xs_ref) = rest[:-3], rest[-3:]
    sub = SUBLANES
    npl = tm // sub
    loc, n = _local(pl.program_id(0), u1, n1, n2)
    g = g_ref[...]

    xs_ref[...] = jnp.swapaxes(x_ref[...].reshape(sub, npl, D_MODEL), 0, 1).reshape(tm, D_MODEL)
    h_ref[0:tm, :] = _rms(xs_ref[...], g).astype(BF16)
    hp = jnp.where(loc == 0, 0.0, _rms(xp_ref[...], g))[sub - 1:sub]
    hn = jnp.where(loc == n - 1, 0.0, _rms(xn_ref[...], g))[0:1]
    hrow = lax.broadcasted_iota(jnp.int32, (2 * sub, D_MODEL), 0)
    h_ref[tm:tm + 2 * sub, :] = jnp.where(hrow == 0, hp, jnp.where(hrow == 1, hn, 0.0)).astype(BF16)

    def up(c0, wd):
        h = h_ref[...]
        return (jnp.dot(h, wu_ref[:, c0:c0 + wd], preferred_element_type=F32),
                jnp.dot(h, wu_ref[:, D_FF + c0:D_FF + c0 + wd], preferred_element_type=F32))

    def conv(u, c0):
        wd = u.shape[1]
        row = lax.broadcasted_iota(jnp.int32, (sub, wd), 0)
        first = jnp.where(row == 0, u[tm:tm + 1], pltpu.roll(u[tm - sub:tm], 1, axis=0))
        last = jnp.where(row == sub - 1, u[tm + 1:tm + 2], pltpu.roll(u[0:sub], sub - 1, axis=0))
        before = jnp.concatenate([first, u[0:tm - sub]], axis=0)
        after = jnp.concatenate([u[sub:tm], last], axis=0)
        return (before * cw_ref[0:1, c0:c0 + wd] + u[0:tm] * cw_ref[1:2, c0:c0 + wd]
                + after * cw_ref[2:3, c0:c0 + wd])

    tiles = [(c0, min(FFN_TN, D_FF - c0)) for c0 in range(0, D_FF, FFN_TN)]
    nxt = up(*tiles[0])
    for t, (c0, wd) in enumerate(tiles):
        ug, uv = nxt
        if t + 1 < len(tiles):
            nxt = up(*tiles[t + 1])
        a_ref[:, c0:c0 + wd] = (_gelu_tanh(conv(ug, c0)) * conv(uv, D_FF + c0)).astype(BF16)

    y = jnp.dot(a_ref[...], wd_ref[...], preferred_element_type=F32)
    y = xs_ref[...] + y
    if final:
        y = _rms(y, gf_ref[...])
    y = jnp.swapaxes(y.reshape(npl, sub, D_MODEL), 0, 1).reshape(tm, D_MODEL)
    if len(o_refs) == 1:
        o_refs[0][...] = y
    else:
        @pl.when(pl.program_id(0) < u1)
        def _():
            o_refs[0][...] = y

        @pl.when(pl.program_id(0) >= u1)
        def _():
            o_refs[1][...] = y


def _ffn(x, layer, g, w_up, conv_w, w_down, gf, tm, u1, n1, n2, final):
    T = x.shape[0]
    hb = tm // SUBLANES
    last = T // SUBLANES - 1
    resident = dict(pipeline_mode=pl.Buffered(1))
    if final:
        out_specs = [pl.BlockSpec((tm, D_MODEL), lambda i: (jnp.minimum(i, u1 - 1), 0)),
                     pl.BlockSpec((tm, D_MODEL), lambda i: (jnp.maximum(i - u1, 0), 0))]
        out_shape = [jax.ShapeDtypeStruct((u1 * tm, D_MODEL), F32),
                     jax.ShapeDtypeStruct((T - u1 * tm, D_MODEL), F32)]
    else:
        out_specs = pl.BlockSpec((tm, D_MODEL), lambda i: (i, 0))
        out_shape = jax.ShapeDtypeStruct((T, D_MODEL), F32)
    return pl.pallas_call(
        functools.partial(_ffn_kernel, tm=tm, u1=u1, n1=n1, n2=n2, final=final),
        grid=(T // tm,),
        in_specs=[
            pl.BlockSpec((SUBLANES, D_MODEL), lambda i: (jnp.maximum(i * hb - 1, 0), 0)),
            pl.BlockSpec((tm, D_MODEL), lambda i: (i, 0)),
            pl.BlockSpec((SUBLANES, D_MODEL), lambda i: (jnp.minimum((i + 1) * hb, last), 0)),
            _layer_spec((1, D_MODEL), layer),
            _layer_spec((D_MODEL, 2 * D_FF), layer, **resident),
            _layer_spec((3, 2 * D_FF), layer, **resident),
            _layer_spec((D_FF, D_MODEL), layer, **resident),
            pl.BlockSpec((1, D_MODEL), lambda i: (0, 0)),
        ],
        out_specs=out_specs,
        out_shape=out_shape,
        scratch_shapes=[
            pltpu.VMEM((tm + 2 * SUBLANES, D_MODEL), BF16),
            pltpu.VMEM((tm, D_FF), BF16),
            pltpu.VMEM((tm, D_MODEL), F32),
        ],
        compiler_params=_cparams(("arbitrary",)),
        name="ffn",
    )(x, x, x, g, w_up, conv_w, w_down, gf)


def _rope_tables(seq_len):
    inv_freq = ROPE_BASE ** (-jnp.arange(0, RET_QK_DIM, 2, dtype=F32) / RET_QK_DIM)
    ang = jnp.arange(seq_len, dtype=F32)[:, None] * inv_freq[None, :]
    cos, sin = jnp.cos(ang), jnp.sin(ang)
    return jnp.concatenate([cos, cos], axis=1), jnp.concatenate([-sin, sin], axis=1)


def _pick_tile(limit, *sizes):
    t = limit
    while any(s % t for s in sizes):
        t //= 2
    return t


def kernel(x_prompt, x_sample, norm_mix_g, w_in, na_rel_bias, ret_decay_fwd, ret_decay_bwd, ret_norm_g,
           w_branch_attn, w_branch_ret, w_out, norm_ffn_g, w_up, ffn_conv_w, w_down, norm_final_g):
    B1, S1, D = x_prompt.shape
    B2, S2, _ = x_sample.shape
    T1, T2 = B1 * S1, B2 * S2
    T = T1 + T2
    depth = w_in.shape[0]
    assert D == D_MODEL and w_in.shape[2] == D_IN
    assert S1 % (NA_WH * GRID_W) == 0 and S2 % (NA_WH * GRID_W) == 0 and T1 % S2 == 0

    xs = (x_prompt.reshape(T1, D), x_sample.reshape(T2, D))
    cos2, sin2 = _rope_tables(max(S1, S2))

    tm_proj = _pick_tile(512, S1, S2)
    tm_mix = _pick_tile(512, S1, S2)

    def units(t):
        return T1 // t, S1 // t, S2 // t

    def gain(g):
        return g.astype(F32)[:, None, :]

    w_in_b, w_up_b, w_down_b = w_in.astype(BF16), w_up.astype(BF16), w_down.astype(BF16)
    wa_b, wr_b, wo_b = w_branch_attn.astype(BF16), w_branch_ret.astype(BF16), w_out.astype(BF16)
    g_mix, g_ffn, g_ret, conv_w = gain(norm_mix_g), gain(norm_ffn_g), gain(ret_norm_g), ffn_conv_w.astype(F32)
    tabs = jax.vmap(_ret_tables)(ret_decay_fwd, ret_decay_bwd)
    heads = na_rel_bias.shape[1]
    tab = _na_bias_table(na_rel_bias.reshape((depth * heads,) + na_rel_bias.shape[2:]))
    tab = tab.reshape((depth, heads // 2) + tab.shape[1:])

    for l in range(depth):
        na, rqk, rv, gts, sb = _inproj(xs, l, g_mix, w_in_b, cos2, sin2, tabs["cd"][l], tabs["kdb"],
                                       tm_proj, *units(tm_proj))
        a = (_na(na, l, tab, 0, B1, S1), _na(na, l, tab, T1, B2, S2))
        x = _mix(xs, a, rqk, rv, sb, gts, l, tabs["cd"][l], g_ret, tabs, wa_b, wr_b, wo_b,
                 *units(RET_CHUNK * RET_CHUNKS_PER_STEP))
        x = _ffn(x, l, g_ffn, w_up_b, conv_w, w_down_b, norm_final_g[None].astype(F32), tm_mix, *units(tm_mix),
                 final=(l == depth - 1))
        xs = (x,)

    y1, y2 = x
    return y1.reshape(B1, S1, D), y2.reshape(B2, S2, D)
```
